```python
import math
import jax, jax.numpy as jnp
from jax import lax
import numpy as np

D_MODEL = 2048
BATCH = 4
SEQ = 2048
DEPTH = 1
DEC_BATCH = 128
DEC_SEQ = 4
PAST_LEN = 16384
PAGE_SIZE = 128

MIX_WIDTH = D_MODEL
POOL_WIDTH = MIX_WIDTH // 2
RET_WIDTH = MIX_WIDTH - POOL_WIDTH
POOL_WINDOWS = (2, 4, 8, 16)
N_POOL_GROUPS = len(POOL_WINDOWS)
POOL_GROUP_DIM = POOL_WIDTH // N_POOL_GROUPS
POOL_BUF = max(POOL_WINDOWS) - 1
RET_HEADS = 4
RET_HEAD_DIM = RET_WIDTH // RET_HEADS
RET_CHUNK = 128
ROPE_BASE = 10000.0
D_FF = 5632
N_MOD = 9
EPS = 1e-6
IN_COLS = POOL_WIDTH + 4 * RET_WIDTH

kernel_name = "hymba_pool_retnet_macaron_adaln_step"


def rmsnorm(x, gain):
    x32 = x.astype(jnp.float32)
    y = x32 * lax.rsqrt(jnp.mean(x32 * x32, axis=-1, keepdims=True) + EPS)
    return (y * gain.astype(jnp.float32)).astype(x.dtype)


def modulate(h, shift, scale):
    return h * (1 + scale[:, None, :]) + shift[:, None, :]


def swiglu(h, w_gate, w_up, w_down):
    return (jax.nn.silu(h @ w_gate) * (h @ w_up)) @ w_down


def rotary(x, pos):
    half = x.shape[-1] // 2
    inv = ROPE_BASE ** (-jnp.arange(half, dtype=jnp.float32) / half)
    ang = pos[:, None] * inv[None, :]
    cos = jnp.cos(ang)[None, :, None, :]
    sin = jnp.sin(ang)[None, :, None, :]
    x1, x2 = x[..., :half], x[..., half:]
    return jnp.concatenate([x1 * cos - x2 * sin, x2 * cos + x1 * sin], axis=-1)


def pool_mixer(u, buf, pos0, pool_w, pool_scale):
    B, T, _ = u.shape
    u32 = u.astype(jnp.float32)
    ue = jnp.concatenate([buf.astype(jnp.float32), u32], axis=1)
    cs = jnp.cumsum(ue, axis=1)
    cs = jnp.concatenate([jnp.zeros_like(cs[:, :1]), cs], axis=1)
    end = POOL_BUF + 1
    pos = jnp.arange(T, dtype=jnp.float32) + pos0
    parts = []
    for g, w in enumerate(POOL_WINDOWS):
        sl = slice(g * POOL_GROUP_DIM, (g + 1) * POOL_GROUP_DIM)
        wsum = cs[:, end:end + T, sl] - cs[:, end - w:end - w + T, sl]
        count = jnp.minimum(pos + 1.0, float(w))
        parts.append(wsum / count[None, :, None] - u32[:, :, sl])
    m = jnp.stack(parts, axis=2)
    out = jnp.einsum('btgc,gcd->btgd', m, pool_w.astype(jnp.float32)).reshape(B, T, POOL_WIDTH)
    out = out * pool_scale.astype(jnp.float32)
    new_buf = ue[:, -POOL_BUF:]
    return out.astype(u.dtype), new_buf.astype(buf.dtype)


def retention(q, k, v, s0, pos0):
    B, T, _ = q.shape
    C = math.gcd(T, RET_CHUNK)
    n = T // C
    pos = jnp.arange(T, dtype=jnp.float32) + pos0

    def heads(t):
        return t.astype(jnp.float32).reshape(B, T, RET_HEADS, RET_HEAD_DIM)

    qh = rotary(heads(q), pos)
    kh = rotary(heads(k), pos) * (RET_HEAD_DIM ** -0.5)
    vh = heads(v)

    def chunks(t):
        return t.reshape(B, n, C, RET_HEADS, RET_HEAD_DIM).transpose(1, 0, 3, 2, 4)

    log_g = jnp.log(1.0 - jnp.power(2.0, -5.0 - jnp.arange(RET_HEADS, dtype=jnp.float32)))
    idx = jnp.arange(C, dtype=jnp.float32)
    diff = idx[:, None] - idx[None, :]
    intra = jnp.where(diff[None] >= 0, jnp.exp(jnp.maximum(diff, 0.0)[None] * log_g[:, None, None]), 0.0)
    q_dec = jnp.exp((idx + 1.0)[None, :] * log_g[:, None])
    k_dec = jnp.exp((C - 1.0 - idx)[None, :] * log_g[:, None])
    c_dec = jnp.exp(C * log_g)

    def step(S, xs):
        qc, kc, vc = xs
        scores = jnp.einsum('bhnd,bhmd->bhnm', qc, kc) * intra[None]
        o = jnp.einsum('bhnm,bhmv->bhnv', scores, vc)
        o = o + jnp.einsum('bhnd,bhdv->bhnv', qc, S) * q_dec[None, :, :, None]
        S = S * c_dec[None, :, None, None] + jnp.einsum('bhmd,bhmv->bhdv', kc * k_dec[None, :, :, None], vc)
        return S, o

    S, o = lax.scan(step, s0.astype(jnp.float32), (chunks(qh), chunks(kh), chunks(vh)))
    o = o.transpose(1, 0, 3, 2, 4).reshape(B, T, RET_HEADS, RET_HEAD_DIM)
    o = o * lax.rsqrt(jnp.mean(o * o, axis=-1, keepdims=True) + EPS)
    return o.reshape(B, T, RET_WIDTH), S.astype(s0.dtype)


def layer(x, c, pool_buf, ret_state, pos0, lw):
    (ada_w, ada_b, norm_ffn1, ffn1_w_gate, ffn1_w_up, ffn1_w_down, norm_mix, w_in, pool_w,
     pool_scale, w_out, norm_ffn2, ffn2_w_gate, ffn2_w_up, ffn2_w_down) = lw
    mods = jax.nn.silu(c) @ ada_w + ada_b
    sh1, sc1, gt1, sh2, sc2, gt2, sh3, sc3, gt3 = jnp.split(mods, N_MOD, axis=-1)
    h = modulate(rmsnorm(x, norm_ffn1), sh1, sc1)
    x = x + 0.5 * gt1[:, None, :] * swiglu(h, ffn1_w_gate, ffn1_w_up, ffn1_w_down)
    h = modulate(rmsnorm(x, norm_mix), sh2, sc2)
    proj = h @ w_in
    u, q, k, v, g = jnp.split(proj, [POOL_WIDTH, POOL_WIDTH + RET_WIDTH, POOL_WIDTH + 2 * RET_WIDTH,
                                     POOL_WIDTH + 3 * RET_WIDTH], axis=-1)
    pool_out, new_buf = pool_mixer(u, pool_buf, pos0, pool_w, pool_scale)
    ret_out, new_state = retention(q, k, v, ret_state, pos0)
    ret_out = (jax.nn.silu(g.astype(jnp.float32)) * ret_out).astype(x.dtype)
    mix = jnp.concatenate([pool_out, ret_out], axis=-1) @ w_out
    x = x + gt2[:, None, :] * mix
    h = modulate(rmsnorm(x, norm_ffn2), sh3, sc3)
    x = x + 0.5 * gt3[:, None, :] * swiglu(h, ffn2_w_gate, ffn2_w_up, ffn2_w_down)
    return x, new_buf, new_state


def trunk(x, c, pool_state, ret_state, pos0, weights, norm_final):
    new_pool, new_ret = [], []
    for l in range(DEPTH):
        lw = tuple(w[l] for w in weights)
        x, pb, rs = layer(x, c, pool_state[l], ret_state[l], pos0, lw)
        new_pool.append(pb)
        new_ret.append(rs)
    y = rmsnorm(x, norm_final)
    return y, jnp.stack(new_pool), jnp.stack(new_ret)


def setup_inputs(seed: int = 0) -> dict:
    key = jax.random.key(seed)
    ks = jax.random.split(key, 24)
    f = jnp.float32
    D, F = D_MODEL, D_FF

    def nrm(k, shape, scale):
        return jax.random.normal(k, shape, f) * scale

    def gain(k, shape):
        return 1.0 + 0.05 * jax.random.normal(k, shape, f)

    return {
        "x_prompt": nrm(ks[0], (BATCH, SEQ, D), 1.0),
        "x_sample": nrm(ks[1], (DEC_BATCH, DEC_SEQ, D), 1.0),
        "c_prompt": nrm(ks[2], (BATCH, D), 1.0),
        "c_sample": nrm(ks[3], (DEC_BATCH, D), 1.0),
        "state_pool": nrm(ks[4], (DEPTH, DEC_BATCH, POOL_BUF, POOL_WIDTH), 1.0),
        "state_ret": nrm(ks[5], (DEPTH, DEC_BATCH, RET_HEADS, RET_HEAD_DIM, RET_HEAD_DIM), 0.5),
        "ada_w": nrm(ks[6], (DEPTH, D, N_MOD * D), 0.3 * D ** -0.5),
        "ada_b": nrm(ks[7], (DEPTH, N_MOD * D), 0.01),
        "norm_ffn1": gain(ks[8], (DEPTH, D)),
        "ffn1_w_gate": nrm(ks[9], (DEPTH, D, F), D ** -0.5),
        "ffn1_w_up": nrm(ks[10], (DEPTH, D, F), D ** -0.5),
        "ffn1_w_down": nrm(ks[11], (DEPTH, F, D), F ** -0.5),
        "norm_mix": gain(ks[12], (DEPTH, D)),
        "w_in": nrm(ks[13], (DEPTH, D, IN_COLS), D ** -0.5),
        "pool_w": nrm(ks[14], (DEPTH, N_POOL_GROUPS, POOL_GROUP_DIM, POOL_GROUP_DIM), POOL_GROUP_DIM ** -0.5),
        "pool_scale": gain(ks[15], (DEPTH, POOL_WIDTH)),
        "w_out": nrm(ks[16], (DEPTH, MIX_WIDTH, D), MIX_WIDTH ** -0.5),
        "norm_ffn2": gain(ks[17], (DEPTH, D)),
        "ffn2_w_gate": nrm(ks[18], (DEPTH, D, F), D ** -0.5),
        "ffn2_w_up": nrm(ks[19], (DEPTH, D, F), D ** -0.5),
        "ffn2_w_down": nrm(ks[20], (DEPTH, F, D), F ** -0.5),
        "norm_final": gain(ks[21], (D,)),
    }


def reference(x_prompt, x_sample, c_prompt, c_sample, state_pool, state_ret,
              ada_w, ada_b, norm_ffn1, ffn1_w_gate, ffn1_w_up, ffn1_w_down,
              norm_mix, w_in, pool_w, pool_scale, w_out,
              norm_ffn2, ffn2_w_gate, ffn2_w_up, ffn2_w_down, norm_final):
    weights = (ada_w, ada_b, norm_ffn1, ffn1_w_gate, ffn1_w_up, ffn1_w_down, norm_mix, w_in,
               pool_w, pool_scale, w_out, norm_ffn2, ffn2_w_gate, ffn2_w_up, ffn2_w_down)
    bp = x_prompt.shape[0]
    pool0 = jnp.zeros((DEPTH, bp, POOL_BUF, POOL_WIDTH), state_pool.dtype)
    ret0 = jnp.zeros((DEPTH, bp, RET_HEADS, RET_HEAD_DIM, RET_HEAD_DIM), state_ret.dtype)
    y_prompt, pool_prompt, ret_prompt = trunk(x_prompt, c_prompt, pool0, ret0, 0, weights, norm_final)
    y_sample, pool_sample, ret_sample = trunk(x_sample, c_sample, state_pool, state_ret, PAST_LEN,
                                              weights, norm_final)
    return (y_prompt, y_sample, pool_prompt, ret_prompt, pool_sample, ret_sample)
```

```python
import functools
import math

import jax
import jax.numpy as jnp
from jax import lax
from jax.experimental import pallas as pl
from jax.experimental.pallas import tpu as pltpu

F32 = jnp.float32
BF16 = jnp.bfloat16

D_MODEL = 2048
D_FF = 5632
N_MOD = 9
EPS = 1e-6
POOL_WINDOWS = (2, 4, 8, 16)
POOL_WIDTH = D_MODEL // 2
POOL_GROUP_DIM = POOL_WIDTH // len(POOL_WINDOWS)
POOL_BUF = max(POOL_WINDOWS) - 1
RET_WIDTH = D_MODEL - POOL_WIDTH
RET_HEADS = 4
RET_HEAD_DIM = RET_WIDTH // RET_HEADS
RET_CHUNK = 128
ROPE_BASE = 10000.0
IN_COLS = POOL_WIDTH + 4 * RET_WIDTH
PAST_LEN = 16384

TILE_P = 1024
N_TILES = 8
VMEM_LIMIT = 60 * 1024 * 1024


def _rms_mod(x, gain, shift, scale):
    y = x * lax.rsqrt(jnp.mean(x * x, axis=-1, keepdims=True) + EPS) * gain
    return y * (1.0 + scale) + shift


def _silu(x):
    return x * jax.nn.sigmoid(x)


def _params(sem, vmem=VMEM_LIMIT):
    return pltpu.CompilerParams(dimension_semantics=sem, vmem_limit_bytes=vmem)


def _ada_kernel(c_ref, w_ref, b_ref, o_ref):
    a = _silu(c_ref[...]).astype(BF16)
    o_ref[...] = jnp.dot(a, w_ref[...].astype(BF16), preferred_element_type=F32) + b_ref[...]


def _ada_call(c_all, ada_w, ada_b):
    rows = c_all.shape[0]
    n = ada_w.shape[1]
    tn = 1024
    return pl.pallas_call(
        _ada_kernel,
        out_shape=jax.ShapeDtypeStruct((rows, n), F32),
        grid=(n // tn,),
        in_specs=[
            pl.BlockSpec((rows, D_MODEL), lambda j: (0, 0)),
            pl.BlockSpec((D_MODEL, tn), lambda j: (0, j)),
            pl.BlockSpec((1, tn), lambda j: (0, j)),
        ],
        out_specs=pl.BlockSpec((rows, tn), lambda j: (0, j)),
        compiler_params=_params(("arbitrary",)),
        name="ada_mods",
    )(c_all, ada_w, ada_b)


def _ffn_kernel(xp_ref, xs_ref, shp_ref, scp_ref, gtp_ref, shs_ref, scs_ref, gts_ref,
                gain_ref, wg_ref, wu_ref, wd_ref, *rest, final_norm):
    if final_norm:
        gfin_ref, op_ref, os_ref, h_ref = rest
    else:
        op_ref, os_ref, h_ref = rest
    f = pl.program_id(1)

    @pl.when(f == 0)
    def _():
        gain = gain_ref[...]
        h_ref[0:TILE_P, :] = _rms_mod(xp_ref[...], gain, shp_ref[0], scp_ref[0]).astype(BF16)
        h_ref[TILE_P:, :] = _rms_mod(xs_ref[...], gain, shs_ref[...], scs_ref[...]).astype(BF16)
        op_ref[...] = jnp.zeros_like(op_ref)
        os_ref[...] = jnp.zeros_like(os_ref)

    h = h_ref[...]
    g = jnp.dot(h, wg_ref[...].astype(BF16), preferred_element_type=F32)
    u = jnp.dot(h, wu_ref[...].astype(BF16), preferred_element_type=F32)
    a = (_silu(g) * u).astype(BF16)
    wd = wd_ref[...].astype(BF16)
    op_ref[...] += jnp.dot(a[0:TILE_P], wd, preferred_element_type=F32)
    os_ref[...] += jnp.dot(a[TILE_P:], wd, preferred_element_type=F32)

    @pl.when(f == pl.num_programs(1) - 1)
    def _():
        yp = xp_ref[...] + 0.5 * gtp_ref[0] * op_ref[...]
        ys = xs_ref[...] + 0.5 * gts_ref[...] * os_ref[...]
        if final_norm:
            gfin = gfin_ref[...]
            yp = yp * lax.rsqrt(jnp.mean(yp * yp, axis=-1, keepdims=True) + EPS) * gfin
            ys = ys * lax.rsqrt(jnp.mean(ys * ys, axis=-1, keepdims=True) + EPS) * gfin
        op_ref[...] = yp
        os_ref[...] = ys


def _ffn_call(xp, xs, mods_p, mods_s, mod0, gain, wg, wu, wd, gain_final=None):
    tf = 256
    ts = xs.shape[0] // N_TILES
    final_norm = gain_final is not None

    def pmod(k):
        return pl.BlockSpec((1, 1, D_MODEL), lambda i, f, k=k: (i // 2, 0, k))

    def smod(k):
        return pl.BlockSpec((ts, D_MODEL), lambda i, f, k=k: (i % 2, k))

    in_specs = [
        pl.BlockSpec((TILE_P, D_MODEL), lambda i, f: (i, 0), pipeline_mode=pl.Buffered(1)),
        pl.BlockSpec((ts, D_MODEL), lambda i, f: (i, 0)),
        pmod(mod0), pmod(mod0 + 1), pmod(mod0 + 2),
        smod(mod0), smod(mod0 + 1), smod(mod0 + 2),
        pl.BlockSpec((1, D_MODEL), lambda i, f: (0, 0)),
        pl.BlockSpec((D_MODEL, tf), lambda i, f: (0, f)),
        pl.BlockSpec((D_MODEL, tf), lambda i, f: (0, f)),
        pl.BlockSpec((tf, D_MODEL), lambda i, f: (f, 0)),
    ]
    args = [xp, xs, mods_p, mods_p, mods_p, mods_s, mods_s, mods_s, gain, wg, wu, wd]
    if final_norm:
        in_specs.append(pl.BlockSpec((1, D_MODEL), lambda i, f: (0, 0)))
        args.append(gain_final)
    return pl.pallas_call(
        functools.partial(_ffn_kernel, final_norm=final_norm),
        out_shape=(jax.ShapeDtypeStruct(xp.shape, F32), jax.ShapeDtypeStruct(xs.shape, F32)),
        grid=(N_TILES, D_FF // tf),
        in_specs=in_specs,
        out_specs=(
            pl.BlockSpec((TILE_P, D_MODEL), lambda i, f: (i, 0)),
            pl.BlockSpec((ts, D_MODEL), lambda i, f: (i, 0)),
        ),
        scratch_shapes=[pltpu.VMEM((TILE_P + ts, D_MODEL), BF16)],
        compiler_params=_params(("arbitrary", "arbitrary")),
        name="ffn_final" if final_norm else "ffn",
    )(*args)


def _proj_kernel(xp_ref, xs_ref, shp_ref, scp_ref, shs_ref, scs_ref, gain_ref, w_ref,
                 op_ref, os_ref, h_ref):
    @pl.when(pl.program_id(1) == 0)
    def _():
        gain = gain_ref[...]
        h_ref[0:TILE_P, :] = _rms_mod(xp_ref[...], gain, shp_ref[0], scp_ref[0]).astype(BF16)
        h_ref[TILE_P:, :] = _rms_mod(xs_ref[...], gain, shs_ref[...], scs_ref[...]).astype(BF16)

    w = w_ref[...].astype(BF16)
    op_ref[...] = jnp.dot(h_ref[0:TILE_P, :], w, preferred_element_type=F32)
    os_ref[...] = jnp.dot(h_ref[TILE_P:, :], w, preferred_element_type=F32)


def _proj_call(xp, xs, mods_p, mods_s, gain, w_in):
    tn = 512
    ts = xs.shape[0] // N_TILES
    return pl.pallas_call(
        _proj_kernel,
        out_shape=(jax.ShapeDtypeStruct((xp.shape[0], IN_COLS), F32),
                   jax.ShapeDtypeStruct((xs.shape[0], IN_COLS), F32)),
        grid=(N_TILES, IN_COLS // tn),
        in_specs=[
            pl.BlockSpec((TILE_P, D_MODEL), lambda i, n: (i, 0)),
            pl.BlockSpec((ts, D_MODEL), lambda i, n: (i, 0)),
            pl.BlockSpec((1, 1, D_MODEL), lambda i, n: (i // 2, 0, 3)),
            pl.BlockSpec((1, 1, D_MODEL), lambda i, n: (i // 2, 0, 4)),
            pl.BlockSpec((ts, D_MODEL), lambda i, n: (i % 2, 3)),
            pl.BlockSpec((ts, D_MODEL), lambda i, n: (i % 2, 4)),
            pl.BlockSpec((1, D_MODEL), lambda i, n: (0, 0)),
            pl.BlockSpec((D_MODEL, tn), lambda i, n: (0, n)),
        ],
        out_specs=(
            pl.BlockSpec((TILE_P, tn), lambda i, n: (i, n)),
            pl.BlockSpec((ts, tn), lambda i, n: (i, n)),
        ),
        scratch_shapes=[pltpu.VMEM((TILE_P + ts, D_MODEL), BF16)],
        compiler_params=_params(("arbitrary", "arbitrary")),
        name="mix_proj",
    )(xp, xs, mods_p, mods_p, mods_s, mods_s, gain, w_in)


POOL_TT = 512
POOL_HALO = 16


def _pool_prompt_kernel(u_ref, halo_ref, pw_ref, ps_ref, o_ref):
    i = pl.program_id(1)
    halo = jnp.where(i == 0, 0.0, halo_ref[...])
    u = u_ref[...]
    ue = jnp.concatenate([halo, u], axis=0)
    pos = (lax.broadcasted_iota(jnp.int32, (POOL_TT, POOL_GROUP_DIM), 0) + i * POOL_TT).astype(F32)
    for g, w in enumerate(POOL_WINDOWS):
        cols = slice(g * POOL_GROUP_DIM, (g + 1) * POOL_GROUP_DIM)
        s = ue[:, cols]
        off = 0
        k = 1
        while k < w:
            s = s[k:, :] + s[:-k, :]
            off += k
            k *= 2
        wsum = s[POOL_HALO - off:POOL_HALO - off + POOL_TT, :]
        count = jnp.minimum(pos + 1.0, float(w))
        m = wsum / count - u[:, cols]
        out = jnp.dot(m.astype(BF16), pw_ref[g].astype(BF16), preferred_element_type=F32)
        o_ref[:, cols] = (out * ps_ref[:, cols]).astype(BF16)


def _pool_prompt_call(proj_p, batch, seq, pool_w, pool_scale):
    tiles = seq // POOL_TT
    halo_per_tile = POOL_TT // POOL_HALO
    halo_per_seq = seq // POOL_HALO
    return pl.pallas_call(
        _pool_prompt_kernel,
        out_shape=jax.ShapeDtypeStruct((batch * seq, POOL_WIDTH), BF16),
        grid=(batch, tiles),
        in_specs=[
            pl.BlockSpec((POOL_TT, POOL_WIDTH), lambda b, i: (b * tiles + i, 0)),
            pl.BlockSpec((POOL_HALO, POOL_WIDTH),
                         lambda b, i: (jnp.maximum(b * halo_per_seq + i * halo_per_tile - 1, 0), 0)),
            pl.BlockSpec(pool_w.shape, lambda b, i: (0, 0, 0)),
            pl.BlockSpec((1, POOL_WIDTH), lambda b, i: (0, 0)),
        ],
        out_specs=pl.BlockSpec((POOL_TT, POOL_WIDTH), lambda b, i: (b * tiles + i, 0)),
        compiler_params=_params(("arbitrary", "arbitrary")),
        name="pool_prompt",
    )(proj_p, proj_p, pool_w, pool_scale)


def _pool_sample_kernel(st_ref, u_ref, pw_ref, ps_ref, o_ref, *, steps, pos0):
    bb = st_ref.shape[0]
    rows = lax.broadcasted_iota(jnp.int32, (bb, POOL_BUF, POOL_GROUP_DIM), 1)
    for g, w in enumerate(POOL_WINDOWS):
        cols = slice(g * POOL_GROUP_DIM, (g + 1) * POOL_GROUP_DIM)
        st = st_ref[:, :, cols]
        us = [u_ref[t, :, cols] for t in range(steps)]
        pw = pw_ref[g].astype(BF16)
        for t in range(steps):
            acc = us[t]
            for t2 in range(max(0, t - w + 1), t):
                acc = acc + us[t2]
            lo = POOL_BUF + t - w + 1
            if lo < POOL_BUF:
                acc = acc + jnp.sum(jnp.where(rows >= lo, st, 0.0), axis=1)
            count = float(min(pos0 + t + 1, w))
            m = acc / count - us[t]
            out = jnp.dot(m.astype(BF16), pw, preferred_element_type=F32)
            o_ref[t, :, cols] = (out * ps_ref[:, cols]).astype(BF16)


def _pool_sample_call(state_pool, proj_s3, pool_w, pool_scale, pos0):
    steps, nb, _ = proj_s3.shape
    bb = 16
    return pl.pallas_call(
        functools.partial(_pool_sample_kernel, steps=steps, pos0=pos0),
        out_shape=jax.ShapeDtypeStruct((steps, nb, POOL_WIDTH), BF16),
        grid=(nb // bb,),
        in_specs=[
            pl.BlockSpec((bb, POOL_BUF, POOL_WIDTH), lambda i: (i, 0, 0)),
            pl.BlockSpec((steps, bb, POOL_WIDTH), lambda i: (0, i, 0)),
            pl.BlockSpec(pool_w.shape, lambda i: (0, 0, 0)),
            pl.BlockSpec((1, POOL_WIDTH), lambda i: (0, 0)),
        ],
        out_specs=pl.BlockSpec((steps, bb, POOL_WIDTH), lambda i: (0, i, 0)),
        compiler_params=_params(("arbitrary",)),
        name="pool_sample",
    )(state_pool, proj_s3, pool_w, pool_scale)


def _rotary(x, cos, sin):
    half = RET_HEAD_DIM // 2
    x1, x2 = x[:, :half], x[:, half:]
    return jnp.concatenate([x1 * cos - x2 * sin, x2 * cos + x1 * sin], axis=1)


def _group_norm_gate(o, g):
    o = o * lax.rsqrt(jnp.mean(o * o, axis=-1, keepdims=True) + EPS)
    return _silu(g) * o


def _ret_prompt_kernel(cdec_ref, q_ref, k_ref, v_ref, g_ref, cos_ref, sin_ref,
                       intra_ref, qdec_ref, kdec_ref, o_ref, s_ref):
    @pl.when(pl.program_id(1) == 0)
    def _():
        s_ref[...] = jnp.zeros_like(s_ref)

    cos = cos_ref[...]
    sin = sin_ref[...]
    for h in range(RET_HEADS):
        cols = slice(h * RET_HEAD_DIM, (h + 1) * RET_HEAD_DIM)
        qr = _rotary(q_ref[:, cols], cos, sin)
        kr = _rotary(k_ref[:, cols], cos, sin) * (RET_HEAD_DIM ** -0.5)
        qb = qr.astype(BF16)
        kb = kr.astype(BF16)
        vb = v_ref[:, cols].astype(BF16)
        scores = lax.dot_general(qb, kb, (((1,), (1,)), ((), ())),
                                 preferred_element_type=F32) * intra_ref[h]
        state = s_ref[0, h]
        o = jnp.dot(scores.astype(BF16), vb, preferred_element_type=F32)
        o = o + jnp.dot(qb, state.astype(BF16), preferred_element_type=F32) * qdec_ref[h]
        kd = (kr * kdec_ref[h]).astype(BF16)
        s_ref[0, h] = state * cdec_ref[h] + lax.dot_general(
            kd, vb, (((0,), (0,)), ((), ())), preferred_element_type=F32)
        o_ref[:, cols] = _group_norm_gate(o, g_ref[:, cols]).astype(BF16)


def _ret_prompt_call(proj_p, batch, seq, tables):
    cos, sin, intra, qdec, kdec, cdec = tables
    chunk = intra.shape[1]
    n = seq // chunk

    def col(j):
        return pl.BlockSpec((chunk, RET_WIDTH), lambda b, c, j=j: (b * n + c, j))

    return pl.pallas_call(
        _ret_prompt_kernel,
        out_shape=(jax.ShapeDtypeStruct((batch * seq, RET_WIDTH), BF16),
                   jax.ShapeDtypeStruct((batch, RET_HEADS, RET_HEAD_DIM, RET_HEAD_DIM), F32)),
        grid=(batch, n),
        in_specs=[
            pl.BlockSpec(memory_space=pltpu.SMEM),
            col(1), col(2), col(3), col(4),
            pl.BlockSpec((chunk, RET_HEAD_DIM // 2), lambda b, c: (c, 0)),
            pl.BlockSpec((chunk, RET_HEAD_DIM // 2), lambda b, c: (c, 0)),
            pl.BlockSpec(intra.shape, lambda b, c: (0, 0, 0)),
            pl.BlockSpec(qdec.shape, lambda b, c: (0, 0, 0)),
            pl.BlockSpec(kdec.shape, lambda b, c: (0, 0, 0)),
        ],
        out_specs=(
            pl.BlockSpec((chunk, RET_WIDTH), lambda b, c: (b * n + c, 0)),
            pl.BlockSpec((1, RET_HEADS, RET_HEAD_DIM, RET_HEAD_DIM), lambda b, c: (b, 0, 0, 0)),
        ),
        compiler_params=_params(("arbitrary", "arbitrary")),
        name="ret_prompt",
    )(cdec, proj_p, proj_p, proj_p, proj_p, cos, sin, intra, qdec, kdec)


RET_S_BB = 8


def _ret_sample_kernel(cdec_ref, q_ref, k_ref, v_ref, g_ref, s0_ref, cos_ref, sin_ref,
                       intra_ref, qdec_ref, kdec_ref, o_ref, s_ref, *, steps):
    bb = RET_S_BB
    rows = steps * bb
    cos = cos_ref[...]
    sin = sin_ref[...]
    row_b = lax.broadcasted_iota(jnp.int32, (rows, RET_HEAD_DIM), 0) % bb

    def gather(ref, cols):
        return jnp.concatenate([ref[t, :, cols] for t in range(steps)], axis=0)

    for h in range(RET_HEADS):
        cols = slice(h * RET_HEAD_DIM, (h + 1) * RET_HEAD_DIM)
        qr = _rotary(gather(q_ref, cols), cos, sin)
        kr = _rotary(gather(k_ref, cols), cos, sin) * (RET_HEAD_DIM ** -0.5)
        qb = qr.astype(BF16)
        kb = kr.astype(BF16)
        vb = gather(v_ref, cols).astype(BF16)
        scores = lax.dot_general(qb, kb, (((1,), (1,)), ((), ())),
                                 preferred_element_type=F32) * intra_ref[h]
        o = jnp.dot(scores.astype(BF16), vb, preferred_element_type=F32)
        kd = kr * kdec_ref[h]
        inter = jnp.zeros((rows, RET_HEAD_DIM), F32)
        for b in range(bb):
            state = s0_ref[b, h]
            mine = row_b == b
            r = jnp.dot(qb, state.astype(BF16), preferred_element_type=F32)
            inter = jnp.where(mine, r, inter)
            kd_b = jnp.where(mine, kd, 0.0).astype(BF16)
            s_ref[b, h] = state * cdec_ref[h] + lax.dot_general(
                kd_b, vb, (((0,), (0,)), ((), ())), preferred_element_type=F32)
        o = o + inter * qdec_ref[h]
        res = _group_norm_gate(o, gather(g_ref, cols))
        for t in range(steps):
            o_ref[t, :, cols] = res[t * bb:(t + 1) * bb, :].astype(BF16)


def _ret_sample_call(proj_s3, state_ret, tables):
    cos, sin, intra, qdec, kdec, cdec = tables
    steps, nb, _ = proj_s3.shape
    bb = RET_S_BB
    rows = steps * bb

    def col(j):
        return pl.BlockSpec((steps, bb, RET_WIDTH), lambda i, j=j: (0, i, j))

    sblock = pl.BlockSpec((bb, RET_HEADS, RET_HEAD_DIM, RET_HEAD_DIM), lambda i: (i, 0, 0, 0))
    return pl.pallas_call(
        functools.partial(_ret_sample_kernel, steps=steps),
        out_shape=(jax.ShapeDtypeStruct((steps, nb, RET_WIDTH), BF16),
                   jax.ShapeDtypeStruct(state_ret.shape, F32)),
        grid=(nb // bb,),
        in_specs=[
            pl.BlockSpec(memory_space=pltpu.SMEM),
            col(1), col(2), col(3), col(4),
            sblock,
            pl.BlockSpec((rows, RET_HEAD_DIM // 2), lambda i: (0, 0)),
            pl.BlockSpec((rows, RET_HEAD_DIM // 2), lambda i: (0, 0)),
            pl.BlockSpec(intra.shape, lambda i: (0, 0, 0)),
            pl.BlockSpec(qdec.shape, lambda i: (0, 0, 0)),
            pl.BlockSpec(kdec.shape, lambda i: (0, 0, 0)),
        ],
        out_specs=(
            pl.BlockSpec((steps, bb, RET_WIDTH), lambda i: (0, i, 0)),
            sblock,
        ),
        compiler_params=_params(("arbitrary",)),
        name="ret_sample",
    )(cdec, proj_s3, proj_s3, proj_s3, proj_s3, state_ret, cos, sin, intra, qdec, kdec)


def _ret_tables(seq, pos0, group):
    chunk = math.gcd(seq, RET_CHUNK)
    half = RET_HEAD_DIM // 2
    pos = jnp.arange(seq, dtype=F32) + pos0
    inv = ROPE_BASE ** (-jnp.arange(half, dtype=F32) / half)
    ang = pos[:, None] * inv[None, :]
    cos, sin = jnp.cos(ang), jnp.sin(ang)
    log_g = jnp.log(1.0 - jnp.power(2.0, -5.0 - jnp.arange(RET_HEADS, dtype=F32)))
    idx = jnp.arange(chunk, dtype=F32)
    diff = idx[:, None] - idx[None, :]
    intra = jnp.where(diff[None] >= 0, jnp.exp(jnp.maximum(diff, 0.0)[None] * log_g[:, None, None]), 0.0)
    q_dec = jnp.exp((idx + 1.0)[None, :] * log_g[:, None])
    k_dec = jnp.exp((chunk - 1.0 - idx)[None, :] * log_g[:, None])
    c_dec = jnp.exp(chunk * log_g)
    if group > 1:
        assert chunk == seq
        cos = jnp.repeat(cos, group, axis=0)
        sin = jnp.repeat(sin, group, axis=0)
        same = jnp.eye(group, dtype=F32)
        intra = (intra[:, :, None, :, None] * same[None, None, :, None, :]).reshape(
            RET_HEADS, chunk * group, chunk * group)
        q_dec = jnp.repeat(q_dec, group, axis=1)
        k_dec = jnp.repeat(k_dec, group, axis=1)
    rows = q_dec.shape[1]
    q_dec = jnp.broadcast_to(q_dec[:, :, None], (RET_HEADS, rows, RET_HEAD_DIM))
    k_dec = jnp.broadcast_to(k_dec[:, :, None], (RET_HEADS, rows, RET_HEAD_DIM))
    return cos, sin, intra, q_dec, k_dec, c_dec


def _out_proj_kernel(pp_ref, rp_ref, ps_ref, rs_ref, xp_ref, xs_ref, gtp_ref, gts_ref,
                     wt_ref, wb_ref, op_ref, os_ref):
    wt = wt_ref[...].astype(BF16)
    wb = wb_ref[...].astype(BF16)
    mp = jnp.dot(pp_ref[...], wt, preferred_element_type=F32)
    mp = mp + jnp.dot(rp_ref[...], wb, preferred_element_type=F32)
    ms = jnp.dot(ps_ref[...], wt, preferred_element_type=F32)
    ms = ms + jnp.dot(rs_ref[...], wb, preferred_element_type=F32)
    op_ref[...] = xp_ref[...] + gtp_ref[0] * mp
    os_ref[...] = xs_ref[...] + gts_ref[...] * ms


def _out_proj_call(pool_p, ret_p, pool_s, ret_s, xp, xs, mods_p, mods_s, w_out):
    tn = 512
    nn = D_MODEL // tn
    ts = xs.shape[0] // N_TILES
    return pl.pallas_call(
        _out_proj_kernel,
        out_shape=(jax.ShapeDtypeStruct(xp.shape, F32), jax.ShapeDtypeStruct(xs.shape, F32)),
        grid=(N_TILES, nn),
        in_specs=[
            pl.BlockSpec((TILE_P, POOL_WIDTH), lambda i, n: (i, 0)),
            pl.BlockSpec((TILE_P, RET_WIDTH), lambda i, n: (i, 0)),
            pl.BlockSpec((ts, POOL_WIDTH), lambda i, n: (i, 0)),
            pl.BlockSpec((ts, RET_WIDTH), lambda i, n: (i, 0)),
            pl.BlockSpec((TILE_P, tn), lambda i, n: (i, n)),
            pl.BlockSpec((ts, tn), lambda i, n: (i, n)),
            pl.BlockSpec((1, 1, tn), lambda i, n: (i // 2, 0, 5 * nn + n)),
            pl.BlockSpec((ts, tn), lambda i, n: (i % 2, 5 * nn + n)),
            pl.BlockSpec((POOL_WIDTH, tn), lambda i, n: (0, n)),
            pl.BlockSpec((RET_WIDTH, tn), lambda i, n: (1, n)),
        ],
        out_specs=(
            pl.BlockSpec((TILE_P, tn), lambda i, n: (i, n)),
            pl.BlockSpec((ts, tn), lambda i, n: (i, n)),
        ),
        compiler_params=_params(("arbitrary", "arbitrary")),
        name="out_proj",
    )(pool_p, ret_p, pool_s, ret_s, xp, xs, mods_p, mods_s, w_out, w_out)


@jax.jit
def _step(x_prompt, x_sample, c_prompt, c_sample, state_pool, state_ret,
          ada_w, ada_b, norm_ffn1, ffn1_w_gate, ffn1_w_up, ffn1_w_down,
          norm_mix, w_in, pool_w, pool_scale, w_out,
          norm_ffn2, ffn2_w_gate, ffn2_w_up, ffn2_w_down, norm_final):
    bp, seq, d = x_prompt.shape
    bs, steps, _ = x_sample.shape
    depth = ada_w.shape[0]
    assert d == D_MODEL and bp * seq == N_TILES * TILE_P and seq % TILE_P == 0
    assert seq // TILE_P == 2 and (bs * steps) % (2 * N_TILES) == 0 and bs % RET_S_BB == 0

    xp = x_prompt.reshape(bp * seq, d)
    xs = jnp.transpose(x_sample, (1, 0, 2)).reshape(steps * bs, d)
    pad = (-(bs + bp)) % 16
    c_all = jnp.concatenate([c_sample, c_prompt, jnp.zeros((pad, d), F32)], axis=0)
    tab_p = _ret_tables(seq, 0, 1)
    tab_s = _ret_tables(steps, PAST_LEN, RET_S_BB)

    pool_prompt, ret_prompt, pool_sample, ret_sample = [], [], [], []
    for l in range(depth):
        mods = _ada_call(c_all, ada_w[l], ada_b[l][None, :])
        mods_s = mods[:bs]
        mods_p = mods[bs:bs + bp].reshape(bp, 1, N_MOD * d)

        xp, xs = _ffn_call(xp, xs, mods_p, mods_s, 0, norm_ffn1[l][None, :],
                           ffn1_w_gate[l], ffn1_w_up[l], ffn1_w_down[l])

        proj_p, proj_s = _proj_call(xp, xs, mods_p, mods_s, norm_mix[l][None, :], w_in[l])
        proj_s3 = proj_s.reshape(steps, bs, IN_COLS)
        po_p = _pool_prompt_call(proj_p, bp, seq, pool_w[l], pool_scale[l][None, :])
        po_s = _pool_sample_call(state_pool[l], proj_s3, pool_w[l], pool_scale[l][None, :], PAST_LEN)
        ro_p, s_p = _ret_prompt_call(proj_p, bp, seq, tab_p)
        ro_s, s_s = _ret_sample_call(proj_s3, state_ret[l], tab_s)
        xp, xs = _out_proj_call(po_p, ro_p, po_s.reshape(steps * bs, POOL_WIDTH),
                                ro_s.reshape(steps * bs, RET_WIDTH), xp, xs, mods_p, mods_s, w_out[l])

        last = l == depth - 1
        xp, xs = _ffn_call(xp, xs, mods_p, mods_s, 6, norm_ffn2[l][None, :],
                           ffn2_w_gate[l], ffn2_w_up[l], ffn2_w_down[l],
                           gain_final=norm_final[None, :] if last else None)

        u_p = proj_p.reshape(bp, seq, IN_COLS)[:, seq - POOL_BUF:, :POOL_WIDTH]
        u_s = jnp.transpose(proj_s3[:, :, :POOL_WIDTH], (1, 0, 2))
        pool_prompt.append(u_p)
        pool_sample.append(jnp.concatenate([state_pool[l], u_s], axis=1)[:, -POOL_BUF:])
        ret_prompt.append(s_p)
        ret_sample.append(s_s)

    y_prompt = xp.reshape(bp, seq, d)
    y_sample = jnp.transpose(xs.reshape(steps, bs, d), (1, 0, 2))
    return (y_prompt, y_sample, jnp.stack(pool_prompt), jnp.stack(ret_prompt),
            jnp.stack(pool_sample), jnp.stack(ret_sample))


def kernel(x_prompt, x_sample, c_prompt, c_sample, state_pool, state_ret, ada_w, ada_b, norm_ffn1, ffn1_w_gate, ffn1_w_up, ffn1_w_down, norm_mix, w_in, pool_w, pool_scale, w_out, norm_ffn2, ffn2_w_gate, ffn2_w_up, ffn2_w_down, norm_final):
    return _step(x_prompt, x_sample, c_prompt, c_sample, state_pool, state_ret,
                 ada_w, ada_b, norm_ffn1, ffn1_w_gate, ffn1_w_up, ffn1_w_down,
                 norm_mix, w_in, pool_w, pool_scale, w_out,
                 norm_ffn2, ffn2_w_gate, ffn2_w_up, ffn2_w_down, norm_final)
```

```python
import functools
import math

import jax
import jax.numpy as jnp
from jax import lax
from jax.experimental import pallas as pl
from jax.experimental.pallas import tpu as pltpu

F32 = jnp.float32
BF16 = jnp.bfloat16

D_MODEL = 2048
D_FF = 5632
N_MOD = 9
EPS = 1e-6
POOL_WINDOWS = (2, 4, 8, 16)
POOL_WIDTH = D_MODEL // 2
POOL_GROUP_DIM = POOL_WIDTH // len(POOL_WINDOWS)
POOL_BUF = max(POOL_WINDOWS) - 1
RET_WIDTH = D_MODEL - POOL_WIDTH
RET_HEADS = 4
RET_HEAD_DIM = RET_WIDTH // RET_HEADS
RET_CHUNK = 128
ROPE_BASE = 10000.0
IN_COLS = POOL_WIDTH + 4 * RET_WIDTH
PAST_LEN = 16384

TILE_P = 1024
N_TILES = 8
VMEM_LIMIT = 62 * 1024 * 1024


def _rms_mod(x, gain, shift, scale):
    y = x * lax.rsqrt(jnp.mean(x * x, axis=-1, keepdims=True) + EPS) * gain
    return y * (1.0 + scale) + shift


def _silu(x):
    return x * jax.nn.sigmoid(x)


ROW_CHUNK = 128


def _fill_h(h_ref, xp_ref, xs_ref, gain, shp, scp, shs, scs):
    def body(c, carry):
        r = pl.multiple_of(c * ROW_CHUNK, ROW_CHUNK)
        h_ref[pl.ds(r, ROW_CHUNK), :] = _rms_mod(xp_ref[pl.ds(r, ROW_CHUNK), :], gain, shp, scp).astype(BF16)
        return carry

    lax.fori_loop(0, TILE_P // ROW_CHUNK, body, 0)
    h_ref[TILE_P:, :] = _rms_mod(xs_ref[...], gain, shs, scs).astype(BF16)


def _params(sem, vmem=VMEM_LIMIT):
    return pltpu.CompilerParams(dimension_semantics=sem, vmem_limit_bytes=vmem)


def _ada_kernel(c_ref, w_ref, b_ref, os_ref, op_ref):
    a = _silu(c_ref[...]).astype(BF16)
    r = jnp.dot(a, w_ref[...].astype(BF16), preferred_element_type=F32) + b_ref[...]
    ns = os_ref.shape[0]
    os_ref[...] = r[:ns]
    op_ref[...] = r[ns:]


def _ada_call(c_all, n_sample, ada_w, ada_b):
    rows = c_all.shape[0]
    n = ada_w.shape[1]
    tn = 1024
    return pl.pallas_call(
        _ada_kernel,
        out_shape=(jax.ShapeDtypeStruct((n_sample, n), F32),
                   jax.ShapeDtypeStruct((rows - n_sample, n), F32)),
        grid=(n // tn,),
        in_specs=[
            pl.BlockSpec((rows, D_MODEL), lambda j: (0, 0)),
            pl.BlockSpec((D_MODEL, tn), lambda j: (0, j)),
            pl.BlockSpec((1, tn), lambda j: (0, j)),
        ],
        out_specs=(pl.BlockSpec((n_sample, tn), lambda j: (0, j)),
                   pl.BlockSpec((rows - n_sample, tn), lambda j: (0, j))),
        compiler_params=_params(("arbitrary",)),
        name="ada_mods",
    )(c_all, ada_w, ada_b)


def _ffn_kernel(xp_ref, xs_ref, shp_ref, scp_ref, gtp_ref, shs_ref, scs_ref, gts_ref,
                gain_ref, wg_ref, wu_ref, wd_ref, *rest, final_norm):
    if final_norm:
        gfin_ref, op_ref, os_ref, h_ref = rest
    else:
        op_ref, os_ref, h_ref = rest
    f = pl.program_id(1)

    @pl.when(f == 0)
    def _():
        _fill_h(h_ref, xp_ref, xs_ref, gain_ref[...], shp_ref[0], scp_ref[0], shs_ref[...], scs_ref[...])
        op_ref[...] = jnp.zeros_like(op_ref)
        os_ref[...] = jnp.zeros_like(os_ref)

    h = h_ref[...]
    g = jnp.dot(h, wg_ref[...].astype(BF16), preferred_element_type=F32)
    u = jnp.dot(h, wu_ref[...].astype(BF16), preferred_element_type=F32)
    a = (_silu(g) * u).astype(BF16)
    wd = wd_ref[...].astype(BF16)
    op_ref[...] += jnp.dot(a[0:TILE_P], wd, preferred_element_type=F32)
    os_ref[...] += jnp.dot(a[TILE_P:], wd, preferred_element_type=F32)

    @pl.when(f == pl.num_programs(1) - 1)
    def _():
        def finish(x, gate, acc):
            y = x + 0.5 * gate * acc
            if final_norm:
                y = y * lax.rsqrt(jnp.mean(y * y, axis=-1, keepdims=True) + EPS) * gfin_ref[...]
            return y

        gate_p = gtp_ref[0]

        def body(c, carry):
            rows = pl.ds(pl.multiple_of(c * ROW_CHUNK, ROW_CHUNK), ROW_CHUNK)
            op_ref[rows, :] = finish(xp_ref[rows, :], gate_p, op_ref[rows, :])
            return carry

        lax.fori_loop(0, TILE_P // ROW_CHUNK, body, 0)
        os_ref[...] = finish(xs_ref[...], gts_ref[...], os_ref[...])


def _ffn_call(xp, xs, mods_p, mods_s, mod0, gain, wg, wu, wd, gain_final=None):
    tf = 256
    ts = xs.shape[0] // N_TILES
    final_norm = gain_final is not None

    def pmod(k):
        return pl.BlockSpec((1, 1, D_MODEL), lambda i, f, k=k: (i // 2, 0, k))

    def smod(k):
        return pl.BlockSpec((ts, D_MODEL), lambda i, f, k=k: (i % 2, k))

    in_specs = [
        pl.BlockSpec((TILE_P, D_MODEL), lambda i, f: (i, 0)),
        pl.BlockSpec((ts, D_MODEL), lambda i, f: (i, 0)),
        pmod(mod0), pmod(mod0 + 1), pmod(mod0 + 2),
        smod(mod0), smod(mod0 + 1), smod(mod0 + 2),
        pl.BlockSpec((1, D_MODEL), lambda i, f: (0, 0)),
        pl.BlockSpec((D_MODEL, tf), lambda i, f: (0, f)),
        pl.BlockSpec((D_MODEL, tf), lambda i, f: (0, f)),
        pl.BlockSpec((tf, D_MODEL), lambda i, f: (f, 0)),
    ]
    args = [xp, xs, mods_p, mods_p, mods_p, mods_s, mods_s, mods_s, gain, wg, wu, wd]
    if final_norm:
        in_specs.append(pl.BlockSpec((1, D_MODEL), lambda i, f: (0, 0)))
        args.append(gain_final)
    return pl.pallas_call(
        functools.partial(_ffn_kernel, final_norm=final_norm),
        out_shape=(jax.ShapeDtypeStruct(xp.shape, F32), jax.ShapeDtypeStruct(xs.shape, F32)),
        grid=(N_TILES, D_FF // tf),
        in_specs=in_specs,
        out_specs=(
            pl.BlockSpec((TILE_P, D_MODEL), lambda i, f: (i, 0)),
            pl.BlockSpec((ts, D_MODEL), lambda i, f: (i, 0)),
        ),
        scratch_shapes=[pltpu.VMEM((TILE_P + ts, D_MODEL), BF16)],
        compiler_params=_params(("arbitrary", "arbitrary")),
        name="ffn_final" if final_norm else "ffn",
    )(*args)


def _proj_kernel(xp_ref, xs_ref, shp_ref, scp_ref, shs_ref, scs_ref, gain_ref, w_ref,
                 op_ref, os_ref, h_ref):
    @pl.when(pl.program_id(1) == 0)
    def _():
        _fill_h(h_ref, xp_ref, xs_ref, gain_ref[...], shp_ref[0], scp_ref[0], shs_ref[...], scs_ref[...])

    w = w_ref[...].astype(BF16)
    op_ref[...] = jnp.dot(h_ref[0:TILE_P, :], w, preferred_element_type=F32)
    os_ref[...] = jnp.dot(h_ref[TILE_P:, :], w, preferred_element_type=F32)


def _proj_call(xp, xs, mods_p, mods_s, gain, w_in):
    tn = 1024
    ts = xs.shape[0] // N_TILES
    return pl.pallas_call(
        _proj_kernel,
        out_shape=(jax.ShapeDtypeStruct((xp.shape[0], IN_COLS), F32),
                   jax.ShapeDtypeStruct((xs.shape[0], IN_COLS), F32)),
        grid=(N_TILES, IN_COLS // tn),
        in_specs=[
            pl.BlockSpec((TILE_P, D_MODEL), lambda i, n: (i, 0)),
            pl.BlockSpec((ts, D_MODEL), lambda i, n: (i, 0)),
            pl.BlockSpec((1, 1, D_MODEL), lambda i, n: (i // 2, 0, 3)),
            pl.BlockSpec((1, 1, D_MODEL), lambda i, n: (i // 2, 0, 4)),
            pl.BlockSpec((ts, D_MODEL), lambda i, n: (i % 2, 3)),
            pl.BlockSpec((ts, D_MODEL), lambda i, n: (i % 2, 4)),
            pl.BlockSpec((1, D_MODEL), lambda i, n: (0, 0)),
            pl.BlockSpec((D_MODEL, tn), lambda i, n: (0, n)),
        ],
        out_specs=(
            pl.BlockSpec((TILE_P, tn), lambda i, n: (i, n)),
            pl.BlockSpec((ts, tn), lambda i, n: (i, n)),
        ),
        scratch_shapes=[pltpu.VMEM((TILE_P + ts, D_MODEL), BF16)],
        compiler_params=_params(("arbitrary", "arbitrary")),
        name="mix_proj",
    )(xp, xs, mods_p, mods_p, mods_s, mods_s, gain, w_in)


POOL_TT = 512
POOL_HALO = 16


def _pool_prompt_kernel(u_ref, halo_ref, pw_ref, ps_ref, o_ref):
    i = pl.program_id(1)
    halo = jnp.where(i == 0, 0.0, halo_ref[...])
    u = u_ref[...]
    ue = jnp.concatenate([halo, u], axis=0)
    pos = (lax.broadcasted_iota(jnp.int32, (POOL_TT, POOL_GROUP_DIM), 0) + i * POOL_TT).astype(F32)
    for g, w in enumerate(POOL_WINDOWS):
        cols = slice(g * POOL_GROUP_DIM, (g + 1) * POOL_GROUP_DIM)
        s = ue[:, cols]
        off = 0
        k = 1
        while k < w:
            s = s[k:, :] + s[:-k, :]
            off += k
            k *= 2
        wsum = s[POOL_HALO - off:POOL_HALO - off + POOL_TT, :]
        count = jnp.minimum(pos + 1.0, float(w))
        m = wsum / count - u[:, cols]
        out = jnp.dot(m.astype(BF16), pw_ref[g].astype(BF16), preferred_element_type=F32)
        o_ref[:, cols] = (out * ps_ref[:, cols]).astype(BF16)


def _pool_prompt_call(proj_p, batch, seq, pool_w, pool_scale):
    tiles = seq // POOL_TT
    halo_per_tile = POOL_TT // POOL_HALO
    halo_per_seq = seq // POOL_HALO
    return pl.pallas_call(
        _pool_prompt_kernel,
        out_shape=jax.ShapeDtypeStruct((batch * seq, POOL_WIDTH), BF16),
        grid=(batch, tiles),
        in_specs=[
            pl.BlockSpec((POOL_TT, POOL_WIDTH), lambda b, i: (b * tiles + i, 0)),
            pl.BlockSpec((POOL_HALO, POOL_WIDTH),
                         lambda b, i: (jnp.maximum(b * halo_per_seq + i * halo_per_tile - 1, 0), 0)),
            pl.BlockSpec(pool_w.shape, lambda b, i: (0, 0, 0)),
            pl.BlockSpec((1, POOL_WIDTH), lambda b, i: (0, 0)),
        ],
        out_specs=pl.BlockSpec((POOL_TT, POOL_WIDTH), lambda b, i: (b * tiles + i, 0)),
        compiler_params=_params(("arbitrary", "arbitrary")),
        name="pool_prompt",
    )(proj_p, proj_p, pool_w, pool_scale)


def _pool_sample_kernel(st_ref, u_ref, pw_ref, ps_ref, o_ref, *, steps, pos0):
    bb = st_ref.shape[0]
    rows = lax.broadcasted_iota(jnp.int32, (bb, POOL_BUF, POOL_GROUP_DIM), 1)
    for g, w in enumerate(POOL_WINDOWS):
        cols = slice(g * POOL_GROUP_DIM, (g + 1) * POOL_GROUP_DIM)
        st = st_ref[:, :, cols]
        us = [u_ref[t, :, cols] for t in range(steps)]
        pw = pw_ref[g].astype(BF16)
        for t in range(steps):
            acc = us[t]
            for t2 in range(max(0, t - w + 1), t):
                acc = acc + us[t2]
            lo = POOL_BUF + t - w + 1
            if lo < POOL_BUF:
                acc = acc + jnp.sum(jnp.where(rows >= lo, st, 0.0), axis=1)
            count = float(min(pos0 + t + 1, w))
            m = acc / count - us[t]
            out = jnp.dot(m.astype(BF16), pw, preferred_element_type=F32)
            o_ref[t, :, cols] = (out * ps_ref[:, cols]).astype(BF16)


def _pool_sample_call(state_pool, proj_s3, pool_w, pool_scale, pos0):
    steps, nb, _ = proj_s3.shape
    bb = 16
    return pl.pallas_call(
        functools.partial(_pool_sample_kernel, steps=steps, pos0=pos0),
        out_shape=jax.ShapeDtypeStruct((steps, nb, POOL_WIDTH), BF16),
        grid=(nb // bb,),
        in_specs=[
            pl.BlockSpec((bb, POOL_BUF, POOL_WIDTH), lambda i: (i, 0, 0)),
            pl.BlockSpec((steps, bb, POOL_WIDTH), lambda i: (0, i, 0)),
            pl.BlockSpec(pool_w.shape, lambda i: (0, 0, 0)),
            pl.BlockSpec((1, POOL_WIDTH), lambda i: (0, 0)),
        ],
        out_specs=pl.BlockSpec((steps, bb, POOL_WIDTH), lambda i: (0, i, 0)),
        compiler_params=_params(("arbitrary",)),
        name="pool_sample",
    )(state_pool, proj_s3, pool_w, pool_scale)


def _rotary(x, cos, sin):
    half = RET_HEAD_DIM // 2
    x1, x2 = x[:, :half], x[:, half:]
    return jnp.concatenate([x1 * cos - x2 * sin, x2 * cos + x1 * sin], axis=1)


def _group_norm_gate(o, g):
    o = o * lax.rsqrt(jnp.mean(o * o, axis=-1, keepdims=True) + EPS)
    return _silu(g) * o


def _ret_prompt_kernel(cdec_ref, q_ref, k_ref, v_ref, g_ref, cos_ref, sin_ref,
                       intra_ref, qdec_ref, kdec_ref, o_ref, s_ref):
    @pl.when(pl.program_id(1) == 0)
    def _():
        s_ref[...] = jnp.zeros_like(s_ref)

    cos = cos_ref[...]
    sin = sin_ref[...]
    for h in range(RET_HEADS):
        cols = slice(h * RET_HEAD_DIM, (h + 1) * RET_HEAD_DIM)
        qr = _rotary(q_ref[:, cols], cos, sin)
        kr = _rotary(k_ref[:, cols], cos, sin) * (RET_HEAD_DIM ** -0.5)
        qb = qr.astype(BF16)
        kb = kr.astype(BF16)
        vb = v_ref[:, cols].astype(BF16)
        scores = lax.dot_general(qb, kb, (((1,), (1,)), ((), ())),
                                 preferred_element_type=F32) * intra_ref[h]
        state = s_ref[0, h]
        o = jnp.dot(scores.astype(BF16), vb, preferred_element_type=F32)
        o = o + jnp.dot(qb, state.astype(BF16), preferred_element_type=F32) * qdec_ref[h]
        kd = (kr * kdec_ref[h]).astype(BF16)
        s_ref[0, h] = state * cdec_ref[h] + lax.dot_general(
            kd, vb, (((0,), (0,)), ((), ())), preferred_element_type=F32)
        o_ref[:, cols] = _group_norm_gate(o, g_ref[:, cols]).astype(BF16)


def _ret_prompt_call(proj_p, batch, seq, tables):
    cos, sin, intra, qdec, kdec, cdec = tables
    chunk = intra.shape[1]
    n = seq // chunk

    def col(j):
        return pl.BlockSpec((chunk, RET_WIDTH), lambda b, c, j=j: (b * n + c, j))

    return pl.pallas_call(
        _ret_prompt_kernel,
        out_shape=(jax.ShapeDtypeStruct((batch * seq, RET_WIDTH), BF16),
                   jax.ShapeDtypeStruct((batch, RET_HEADS, RET_HEAD_DIM, RET_HEAD_DIM), F32)),
        grid=(batch, n),
        in_specs=[
            pl.BlockSpec(memory_space=pltpu.SMEM),
            col(1), col(2), col(3), col(4),
            pl.BlockSpec((chunk, RET_HEAD_DIM // 2), lambda b, c: (c, 0)),
            pl.BlockSpec((chunk, RET_HEAD_DIM // 2), lambda b, c: (c, 0)),
            pl.BlockSpec(intra.shape, lambda b, c: (0, 0, 0)),
            pl.BlockSpec(qdec.shape, lambda b, c: (0, 0, 0)),
            pl.BlockSpec(kdec.shape, lambda b, c: (0, 0, 0)),
        ],
        out_specs=(
            pl.BlockSpec((chunk, RET_WIDTH), lambda b, c: (b * n + c, 0)),
            pl.BlockSpec((1, RET_HEADS, RET_HEAD_DIM, RET_HEAD_DIM), lambda b, c: (b, 0, 0, 0)),
        ),
        compiler_params=_params(("arbitrary", "arbitrary")),
        name="ret_prompt",
    )(cdec, proj_p, proj_p, proj_p, proj_p, cos, sin, intra, qdec, kdec)


RET_S_BB = 8


def _ret_sample_kernel(cdec_ref, q_ref, k_ref, v_ref, g_ref, s0_ref, cos_ref, sin_ref,
                       intra_ref, qdec_ref, kdec_ref, o_ref, s_ref, *, steps):
    bb = RET_S_BB
    rows = steps * bb
    cos = cos_ref[...]
    sin = sin_ref[...]
    row_b = lax.broadcasted_iota(jnp.int32, (rows, RET_HEAD_DIM), 0) % bb

    def gather(ref, cols):
        return jnp.concatenate([ref[t, :, cols] for t in range(steps)], axis=0)

    for h in range(RET_HEADS):
        cols = slice(h * RET_HEAD_DIM, (h + 1) * RET_HEAD_DIM)
        qr = _rotary(gather(q_ref, cols), cos, sin)
        kr = _rotary(gather(k_ref, cols), cos, sin) * (RET_HEAD_DIM ** -0.5)
        qb = qr.astype(BF16)
        kb = kr.astype(BF16)
        vb = gather(v_ref, cols).astype(BF16)
        scores = lax.dot_general(qb, kb, (((1,), (1,)), ((), ())),
                                 preferred_element_type=F32) * intra_ref[h]
        o = jnp.dot(scores.astype(BF16), vb, preferred_element_type=F32)
        kd = kr * kdec_ref[h]
        inter = jnp.zeros((rows, RET_HEAD_DIM), F32)
        for b in range(bb):
            state = s0_ref[b, h]
            mine = row_b == b
            r = jnp.dot(qb, state.astype(BF16), preferred_element_type=F32)
            inter = jnp.where(mine, r, inter)
            kd_b = jnp.where(mine, kd, 0.0).astype(BF16)
            s_ref[b, h] = state * cdec_ref[h] + lax.dot_general(
                kd_b, vb, (((0,), (0,)), ((), ())), preferred_element_type=F32)
        o = o + inter * qdec_ref[h]
        res = _group_norm_gate(o, gather(g_ref, cols))
        for t in range(steps):
            o_ref[t, :, cols] = res[t * bb:(t + 1) * bb, :].astype(BF16)


def _ret_sample_call(proj_s3, state_ret, tables):
    cos, sin, intra, qdec, kdec, cdec = tables
    steps, nb, _ = proj_s3.shape
    bb = RET_S_BB
    rows = steps * bb

    def col(j):
        return pl.BlockSpec((steps, bb, RET_WIDTH), lambda i, j=j: (0, i, j))

    sblock = pl.BlockSpec((bb, RET_HEADS, RET_HEAD_DIM, RET_HEAD_DIM), lambda i: (i, 0, 0, 0))
    return pl.pallas_call(
        functools.partial(_ret_sample_kernel, steps=steps),
        out_shape=(jax.ShapeDtypeStruct((steps, nb, RET_WIDTH), BF16),
                   jax.ShapeDtypeStruct(state_ret.shape, F32)),
        grid=(nb // bb,),
        in_specs=[
            pl.BlockSpec(memory_space=pltpu.SMEM),
            col(1), col(2), col(3), col(4),
            sblock,
            pl.BlockSpec((rows, RET_HEAD_DIM // 2), lambda i: (0, 0)),
            pl.BlockSpec((rows, RET_HEAD_DIM // 2), lambda i: (0, 0)),
            pl.BlockSpec(intra.shape, lambda i: (0, 0, 0)),
            pl.BlockSpec(qdec.shape, lambda i: (0, 0, 0)),
            pl.BlockSpec(kdec.shape, lambda i: (0, 0, 0)),
        ],
        out_specs=(
            pl.BlockSpec((steps, bb, RET_WIDTH), lambda i: (0, i, 0)),
            sblock,
        ),
        compiler_params=_params(("arbitrary",)),
        name="ret_sample",
    )(cdec, proj_s3, proj_s3, proj_s3, proj_s3, state_ret, cos, sin, intra, qdec, kdec)


def _ret_tables(seq, pos0, group):
    chunk = math.gcd(seq, RET_CHUNK)
    half = RET_HEAD_DIM // 2
    pos = jnp.arange(seq, dtype=F32) + pos0
    inv = ROPE_BASE ** (-jnp.arange(half, dtype=F32) / half)
    ang = pos[:, None] * inv[None, :]
    cos, sin = jnp.cos(ang), jnp.sin(ang)
    log_g = jnp.log(1.0 - jnp.power(2.0, -5.0 - jnp.arange(RET_HEADS, dtype=F32)))
    idx = jnp.arange(chunk, dtype=F32)
    diff = idx[:, None] - idx[None, :]
    intra = jnp.where(diff[None] >= 0, jnp.exp(jnp.maximum(diff, 0.0)[None] * log_g[:, None, None]), 0.0)
    q_dec = jnp.exp((idx + 1.0)[None, :] * log_g[:, None])
    k_dec = jnp.exp((chunk - 1.0 - idx)[None, :] * log_g[:, None])
    c_dec = jnp.exp(chunk * log_g)
    if group > 1:
        assert chunk == seq
        cos = jnp.repeat(cos, group, axis=0)
        sin = jnp.repeat(sin, group, axis=0)
        same = jnp.eye(group, dtype=F32)
        intra = (intra[:, :, None, :, None] * same[None, None, :, None, :]).reshape(
            RET_HEADS, chunk * group, chunk * group)
        q_dec = jnp.repeat(q_dec, group, axis=1)
        k_dec = jnp.repeat(k_dec, group, axis=1)
    rows = q_dec.shape[1]
    q_dec = jnp.broadcast_to(q_dec[:, :, None], (RET_HEADS, rows, RET_HEAD_DIM))
    k_dec = jnp.broadcast_to(k_dec[:, :, None], (RET_HEADS, rows, RET_HEAD_DIM))
    return cos, sin, intra, q_dec, k_dec, c_dec


def _out_proj_kernel(pp_ref, rp_ref, ps_ref, rs_ref, xp_ref, xs_ref, gtp_ref, gts_ref,
                     wt_ref, wb_ref, op_ref, os_ref):
    wt = wt_ref[...].astype(BF16)
    wb = wb_ref[...].astype(BF16)
    mp = jnp.dot(pp_ref[...], wt, preferred_element_type=F32)
    mp = mp + jnp.dot(rp_ref[...], wb, preferred_element_type=F32)
    ms = jnp.dot(ps_ref[...], wt, preferred_element_type=F32)
    ms = ms + jnp.dot(rs_ref[...], wb, preferred_element_type=F32)
    op_ref[...] = xp_ref[...] + gtp_ref[0] * mp
    os_ref[...] = xs_ref[...] + gts_ref[...] * ms


def _out_proj_call(pool_p, ret_p, pool_s, ret_s, xp, xs, mods_p, mods_s, w_out):
    tn = 1024
    nn = D_MODEL // tn
    ts = xs.shape[0] // N_TILES
    return pl.pallas_call(
        _out_proj_kernel,
        out_shape=(jax.ShapeDtypeStruct(xp.shape, F32), jax.ShapeDtypeStruct(xs.shape, F32)),
        grid=(nn, N_TILES),
        in_specs=[
            pl.BlockSpec((TILE_P, POOL_WIDTH), lambda n, i: (i, 0)),
            pl.BlockSpec((TILE_P, RET_WIDTH), lambda n, i: (i, 0)),
            pl.BlockSpec((ts, POOL_WIDTH), lambda n, i: (i, 0)),
            pl.BlockSpec((ts, RET_WIDTH), lambda n, i: (i, 0)),
            pl.BlockSpec((TILE_P, tn), lambda n, i: (i, n)),
            pl.BlockSpec((ts, tn), lambda n, i: (i, n)),
            pl.BlockSpec((1, 1, tn), lambda n, i: (i // 2, 0, 5 * nn + n)),
            pl.BlockSpec((ts, tn), lambda n, i: (i % 2, 5 * nn + n)),
            pl.BlockSpec((POOL_WIDTH, tn), lambda n, i: (0, n)),
            pl.BlockSpec((RET_WIDTH, tn), lambda n, i: (1, n)),
        ],
        out_specs=(
            pl.BlockSpec((TILE_P, tn), lambda n, i: (i, n)),
            pl.BlockSpec((ts, tn), lambda n, i: (i, n)),
        ),
        compiler_params=_params(("arbitrary", "arbitrary")),
        name="out_proj",
    )(pool_p, ret_p, pool_s, ret_s, xp, xs, mods_p, mods_s, w_out, w_out)


@jax.jit
def _step(x_prompt, x_sample, c_prompt, c_sample, state_pool, state_ret,
          ada_w, ada_b, norm_ffn1, ffn1_w_gate, ffn1_w_up, ffn1_w_down,
          norm_mix, w_in, pool_w, pool_scale, w_out,
          norm_ffn2, ffn2_w_gate, ffn2_w_up, ffn2_w_down, norm_final):
    bp, seq, d = x_prompt.shape
    bs, steps, _ = x_sample.shape
    depth = ada_w.shape[0]
    assert d == D_MODEL and bp * seq == N_TILES * TILE_P and seq % TILE_P == 0
    assert seq // TILE_P == 2 and (bs * steps) % (2 * N_TILES) == 0 and bs % RET_S_BB == 0

    xp = x_prompt.reshape(bp * seq, d)
    xs = jnp.transpose(x_sample, (1, 0, 2)).reshape(steps * bs, d)
    pad = (-(bs + bp)) % 16
    c_all = jnp.concatenate([c_sample, c_prompt, jnp.zeros((pad, d), F32)], axis=0)
    tab_p = _ret_tables(seq, 0, 1)
    tab_s = _ret_tables(steps, PAST_LEN, RET_S_BB)

    pool_prompt, ret_prompt, pool_sample, ret_sample = [], [], [], []
    for l in range(depth):
        mods_s, mods_pp = _ada_call(c_all, bs, ada_w[l], ada_b[l][None, :])
        mods_p = mods_pp[:bp].reshape(bp, 1, N_MOD * d)

        xp, xs = _ffn_call(xp, xs, mods_p, mods_s, 0, norm_ffn1[l][None, :],
                           ffn1_w_gate[l], ffn1_w_up[l], ffn1_w_down[l])

        proj_p, proj_s = _proj_call(xp, xs, mods_p, mods_s, norm_mix[l][None, :], w_in[l])
        proj_s3 = proj_s.reshape(steps, bs, IN_COLS)
        po_p = _pool_prompt_call(proj_p, bp, seq, pool_w[l], pool_scale[l][None, :])
        po_s = _pool_sample_call(state_pool[l], proj_s3, pool_w[l], pool_scale[l][None, :], PAST_LEN)
        ro_p, s_p = _ret_prompt_call(proj_p, bp, seq, tab_p)
        ro_s, s_s = _ret_sample_call(proj_s3, state_ret[l], tab_s)
        xp, xs = _out_proj_call(po_p, ro_p, po_s.reshape(steps * bs, POOL_WIDTH),
                                ro_s.reshape(steps * bs, RET_WIDTH), xp, xs, mods_p, mods_s, w_out[l])

        last = l == depth - 1
        xp, xs = _ffn_call(xp, xs, mods_p, mods_s, 6, norm_ffn2[l][None, :],
                           ffn2_w_gate[l], ffn2_w_up[l], ffn2_w_down[l],
                           gain_final=norm_final[None, :] if last else None)

        u_p = proj_p.reshape(bp, seq, IN_COLS)[:, seq - POOL_BUF:, :POOL_WIDTH]
        u_s = jnp.transpose(proj_s3[:, :, :POOL_WIDTH], (1, 0, 2))
        pool_prompt.append(u_p)
        pool_sample.append(jnp.concatenate([state_pool[l], u_s], axis=1)[:, -POOL_BUF:])
        ret_prompt.append(s_p)
        ret_sample.append(s_s)

    y_prompt = xp.reshape(bp, seq, d)
    y_sample = jnp.transpose(xs.reshape(steps, bs, d), (1, 0, 2))
    return (y_prompt, y_sample, jnp.stack(pool_prompt), jnp.stack(ret_prompt),
            jnp.stack(pool_sample), jnp.stack(ret_sample))


def kernel(x_prompt, x_sample, c_prompt, c_sample, state_pool, state_ret, ada_w, ada_b, norm_ffn1, ffn1_w_gate, ffn1_w_up, ffn1_w_down, norm_mix, w_in, pool_w, pool_scale, w_out, norm_ffn2, ffn2_w_gate, ffn2_w_up, ffn2_w_down, norm_final):
    return _step(x_prompt, x_sample, c_prompt, c_sample, state_pool, state_ret,
                 ada_w, ada_b, norm_ffn1, ffn1_w_gate, ffn1_w_up, ffn1_w_down,
                 norm_mix, w_in, pool_w, pool_scale, w_out,
                 norm_ffn2, ffn2_w_gate, ffn2_w_up, ffn2_w_down, norm_final)
```

```python
import functools
import math

import jax
import jax.numpy as jnp
from jax import lax
from jax.experimental import pallas as pl
from jax.experimental.pallas import tpu as pltpu

F32 = jnp.float32
BF16 = jnp.bfloat16

D_MODEL = 2048
D_FF = 5632
N_MOD = 9
EPS = 1e-6
POOL_WINDOWS = (2, 4, 8, 16)
POOL_WIDTH = D_MODEL // 2
POOL_GROUP_DIM = POOL_WIDTH // len(POOL_WINDOWS)
POOL_BUF = max(POOL_WINDOWS) - 1
MIX_WIDTH = D_MODEL
RET_WIDTH = MIX_WIDTH - POOL_WIDTH
RET_HEADS = 4
RET_HEAD_DIM = RET_WIDTH // RET_HEADS
RET_CHUNK = 128
ROPE_BASE = 10000.0
IN_COLS = POOL_WIDTH + 4 * RET_WIDTH
PAST_LEN = 16384

TILE_P = 1024
N_TILES = 8
VMEM_LIMIT = 62 * 1024 * 1024


def _rms_mod(x, gain, shift, scale):
    y = x * lax.rsqrt(jnp.mean(x * x, axis=-1, keepdims=True) + EPS) * gain
    return y * (1.0 + scale) + shift


def _silu(x):
    return x * jax.nn.sigmoid(x)


ROW_CHUNK = 16
NORM_UNROLL = 8
FINAL_CHUNK = 128


def _fill_h(h_ref, xp_ref, xs_ref, gain, shp, scp, shs, scs, accp_ref=None, accs_ref=None):
    def body(c, carry):
        rows = pl.ds(pl.multiple_of(c * ROW_CHUNK, ROW_CHUNK), ROW_CHUNK)
        x = xp_ref[rows, :]
        h_ref[rows, :] = _rms_mod(x, gain, shp, scp).astype(BF16)
        if accp_ref is not None:
            accp_ref[rows, :] = x
        return carry

    n_p = xp_ref.shape[0]
    lax.fori_loop(0, n_p // ROW_CHUNK, body, 0, unroll=NORM_UNROLL)
    for s in range(0, xs_ref.shape[0], ROW_CHUNK):
        rows = slice(s, s + ROW_CHUNK)
        x = xs_ref[rows, :]
        h_ref[n_p + s:n_p + s + ROW_CHUNK, :] = _rms_mod(x, gain, shs[rows, :], scs[rows, :]).astype(BF16)
        if accs_ref is not None:
            accs_ref[rows, :] = x


def _params(sem, vmem=VMEM_LIMIT):
    return pltpu.CompilerParams(dimension_semantics=sem, vmem_limit_bytes=vmem)


def _ada_kernel(c_ref, w_ref, b_ref, os_ref, op_ref):
    a = _silu(c_ref[...]).astype(BF16)
    r = jnp.dot(a, w_ref[...].astype(BF16), preferred_element_type=F32) + b_ref[...]
    ns = os_ref.shape[0]
    os_ref[...] = r[:ns]
    op_ref[...] = r[ns:]


def _ada_call(c_all, n_sample, ada_w, ada_b):
    rows = c_all.shape[0]
    n = ada_w.shape[1]
    tn = 1024
    return pl.pallas_call(
        _ada_kernel,
        out_shape=(jax.ShapeDtypeStruct((n_sample, n), F32),
                   jax.ShapeDtypeStruct((rows - n_sample, n), F32)),
        grid=(n // tn,),
        in_specs=[
            pl.BlockSpec((rows, D_MODEL), lambda j: (0, 0)),
            pl.BlockSpec((D_MODEL, tn), lambda j: (0, j)),
            pl.BlockSpec((1, tn), lambda j: (0, j)),
        ],
        out_specs=(pl.BlockSpec((n_sample, tn), lambda j: (0, j)),
                   pl.BlockSpec((rows - n_sample, tn), lambda j: (0, j))),
        compiler_params=_params(("arbitrary",)),
        name="ada_mods",
    )(c_all, ada_w, ada_b)


def _ffn_kernel(xp_ref, xs_ref, shp_ref, scp_ref, gtp_ref, shs_ref, scs_ref, gts_ref,
                gain_ref, wg_ref, wu_ref, wd_ref, *rest, final_norm):
    if final_norm:
        gfin_ref, op_ref, os_ref, h_ref = rest
    else:
        op_ref, os_ref, h_ref = rest
    f = pl.program_id(1)

    @pl.when(f == 0)
    def _():
        _fill_h(h_ref, xp_ref, xs_ref, gain_ref[...], shp_ref[0], scp_ref[0], shs_ref, scs_ref,
                accp_ref=op_ref, accs_ref=os_ref)

    h = h_ref[...]
    g = jnp.dot(h, wg_ref[...].astype(BF16), preferred_element_type=F32)
    u = jnp.dot(h, wu_ref[...].astype(BF16), preferred_element_type=F32)
    a = (_silu(g) * u).astype(BF16)
    d = jnp.dot(a, wd_ref[...].astype(BF16), preferred_element_type=F32)
    op_ref[...] += (0.5 * gtp_ref[0]) * d[0:TILE_P]
    os_ref[...] += (0.5 * gts_ref[...]) * d[TILE_P:]

    if final_norm:
        @pl.when(f == pl.num_programs(1) - 1)
        def _():
            gfin = gfin_ref[...]

            def norm(y):
                return y * lax.rsqrt(jnp.mean(y * y, axis=-1, keepdims=True) + EPS) * gfin

            def body(c, carry):
                rows = pl.ds(pl.multiple_of(c * FINAL_CHUNK, FINAL_CHUNK), FINAL_CHUNK)
                op_ref[rows, :] = norm(op_ref[rows, :])
                return carry

            lax.fori_loop(0, TILE_P // FINAL_CHUNK, body, 0)
            os_ref[...] = norm(os_ref[...])


def _ffn_call(xp, xs, mods_p, mods_s, mod0, gain, wg, wu, wd, gain_final=None):
    tf = 256
    ts = xs.shape[0] // N_TILES
    final_norm = gain_final is not None

    def pmod(k):
        return pl.BlockSpec((1, 1, D_MODEL), lambda i, f, k=k: (i // 2, 0, k))

    def smod(k):
        return pl.BlockSpec((ts, D_MODEL), lambda i, f, k=k: (i % 2, k))

    in_specs = [
        pl.BlockSpec((TILE_P, D_MODEL), lambda i, f: (i, 0)),
        pl.BlockSpec((ts, D_MODEL), lambda i, f: (i, 0)),
        pmod(mod0), pmod(mod0 + 1), pmod(mod0 + 2),
        smod(mod0), smod(mod0 + 1), smod(mod0 + 2),
        pl.BlockSpec((1, D_MODEL), lambda i, f: (0, 0)),
        pl.BlockSpec((D_MODEL, tf), lambda i, f: (0, f)),
        pl.BlockSpec((D_MODEL, tf), lambda i, f: (0, f)),
        pl.BlockSpec((tf, D_MODEL), lambda i, f: (f, 0)),
    ]
    args = [xp, xs, mods_p, mods_p, mods_p, mods_s, mods_s, mods_s, gain, wg, wu, wd]
    if final_norm:
        in_specs.append(pl.BlockSpec((1, D_MODEL), lambda i, f: (0, 0)))
        args.append(gain_final)
    return pl.pallas_call(
        functools.partial(_ffn_kernel, final_norm=final_norm),
        out_shape=(jax.ShapeDtypeStruct(xp.shape, F32), jax.ShapeDtypeStruct(xs.shape, F32)),
        grid=(N_TILES, D_FF // tf),
        in_specs=in_specs,
        out_specs=(
            pl.BlockSpec((TILE_P, D_MODEL), lambda i, f: (i, 0)),
            pl.BlockSpec((ts, D_MODEL), lambda i, f: (i, 0)),
        ),
        scratch_shapes=[pltpu.VMEM((TILE_P + ts, D_MODEL), BF16)],
        compiler_params=_params(("arbitrary", "arbitrary")),
        name="ffn_final" if final_norm else "ffn",
    )(*args)


def _rotary(x, cos, sin):
    half = RET_HEAD_DIM // 2
    x1, x2 = x[:, :half], x[:, half:]
    return jnp.concatenate([x1 * cos - x2 * sin, x2 * cos + x1 * sin], axis=1)


PROJ_TN = 1024


def _proj_kernel(xp_ref, xs_ref, shp_ref, scp_ref, shs_ref, scs_ref, gain_ref, w_ref,
                 cosp_ref, sinp_ref, coss_ref, sins_ref, op_ref, os_ref, h_ref):
    n = pl.program_id(1)
    sub = pl.program_id(2)

    @pl.when(n == 0)
    def _():
        _fill_h(h_ref.at[sub], xp_ref, xs_ref, gain_ref[...], shp_ref[0], scp_ref[0], shs_ref, scs_ref)

    def project():
        return jnp.dot(h_ref[sub], w_ref[...].astype(BF16), preferred_element_type=F32)

    rotate = (n == 1) | (n == 2)

    @pl.when(jnp.logical_not(rotate))
    def _():
        r = project()
        op_ref[...] = r[0:TILE_P]
        os_ref[...] = r[TILE_P:]

    @pl.when(rotate)
    def _():
        r = project()
        scale = jnp.where(n == 2, RET_HEAD_DIM ** -0.5, 1.0).astype(F32)
        cos = jnp.concatenate([cosp_ref[...], coss_ref[...]], axis=0)
        sin = jnp.concatenate([sinp_ref[...], sins_ref[...]], axis=0)
        for h in range(RET_HEADS):
            cols = slice(h * RET_HEAD_DIM, (h + 1) * RET_HEAD_DIM)
            y = _rotary(r[:, cols], cos, sin) * scale
            op_ref[:, cols] = y[0:TILE_P]
            os_ref[:, cols] = y[TILE_P:]


def _proj_call(xp, xs, mods_p, mods_s, gain, w_in, rot_p, rot_s):
    tn = PROJ_TN
    assert tn == POOL_WIDTH == RET_WIDTH
    ts = xs.shape[0] // N_TILES
    half = RET_HEAD_DIM // 2

    def tile(p, n, s):
        return 2 * p + jnp.where(n == 0, s, 1)

    def rot_rows(p, n, s):
        return jnp.where((n == 1) | (n == 2), s, 1)

    return pl.pallas_call(
        _proj_kernel,
        out_shape=(jax.ShapeDtypeStruct((xp.shape[0], IN_COLS), F32),
                   jax.ShapeDtypeStruct((xs.shape[0], IN_COLS), F32)),
        grid=(N_TILES // 2, IN_COLS // tn, 2),
        in_specs=[
            pl.BlockSpec((TILE_P, D_MODEL), lambda p, n, s: (tile(p, n, s), 0)),
            pl.BlockSpec((ts, D_MODEL), lambda p, n, s: (tile(p, n, s), 0)),
            pl.BlockSpec((1, 1, D_MODEL), lambda p, n, s: (p, 0, 3)),
            pl.BlockSpec((1, 1, D_MODEL), lambda p, n, s: (p, 0, 4)),
            pl.BlockSpec((ts, D_MODEL), lambda p, n, s: (tile(0, n, s), 3)),
            pl.BlockSpec((ts, D_MODEL), lambda p, n, s: (tile(0, n, s), 4)),
            pl.BlockSpec((1, D_MODEL), lambda p, n, s: (0, 0)),
            pl.BlockSpec((D_MODEL, tn), lambda p, n, s: (0, n)),
            pl.BlockSpec((TILE_P, half), lambda p, n, s: (rot_rows(p, n, s), 0)),
            pl.BlockSpec((TILE_P, half), lambda p, n, s: (rot_rows(p, n, s), 0)),
            pl.BlockSpec((ts, half), lambda p, n, s: (p, 0)),
            pl.BlockSpec((ts, half), lambda p, n, s: (p, 0)),
        ],
        out_specs=(
            pl.BlockSpec((TILE_P, tn), lambda p, n, s: (2 * p + s, n)),
            pl.BlockSpec((ts, tn), lambda p, n, s: (2 * p + s, n)),
        ),
        scratch_shapes=[pltpu.VMEM((2, TILE_P + ts, D_MODEL), BF16)],
        compiler_params=_params(("arbitrary", "arbitrary", "arbitrary")),
        name="mix_proj",
    )(xp, xs, mods_p, mods_p, mods_s, mods_s, gain, w_in, *rot_p, *rot_s)


POOL_HALO = 16


def _group_norm_gate(o, g):
    o = o * lax.rsqrt(jnp.mean(o * o, axis=-1, keepdims=True) + EPS)
    return _silu(g) * o


MIX_TT = 512


def _mixer_prompt_kernel(cdec_ref, proj_ref, halo_ref,
                         intra_ref, qdec_ref, kdec_ref, pw_ref, ps_ref, mo_ref, s_ref, nb_ref):
    i = pl.program_id(1)
    u_ref = proj_ref.at[:, 0:POOL_WIDTH]
    q_ref, k_ref, v_ref, g_ref = (
        proj_ref.at[:, POOL_WIDTH + j * RET_WIDTH:POOL_WIDTH + (j + 1) * RET_WIDTH] for j in range(4))

    @pl.when(i == pl.num_programs(1) - 1)
    def _():
        nb_ref[0] = u_ref[MIX_TT - POOL_BUF:MIX_TT, :]

    @pl.when(i == 0)
    def _():
        s_ref[...] = jnp.zeros_like(s_ref)

    halo = jnp.where(i == 0, 0.0, halo_ref[...])
    u = u_ref[...]
    ue = jnp.concatenate([halo, u], axis=0)
    pos = (lax.broadcasted_iota(jnp.int32, (MIX_TT, POOL_GROUP_DIM), 0) + i * MIX_TT).astype(F32)
    for g, w in enumerate(POOL_WINDOWS):
        cols = slice(g * POOL_GROUP_DIM, (g + 1) * POOL_GROUP_DIM)
        s = ue[:, cols]
        off = 0
        k = 1
        while k < w:
            s = s[k:, :] + s[:-k, :]
            off += k
            k *= 2
        wsum = s[POOL_HALO - off:POOL_HALO - off + MIX_TT, :]
        count = jnp.minimum(pos + 1.0, float(w))
        m = wsum / count - u[:, cols]
        out = jnp.dot(m.astype(BF16), pw_ref[g].astype(BF16), preferred_element_type=F32)
        mo_ref[:, cols] = (out * ps_ref[:, cols]).astype(BF16)

    chunk = intra_ref.shape[1]
    for c in range(MIX_TT // chunk):
        rows = slice(c * chunk, (c + 1) * chunk)
        for h in range(RET_HEADS):
            cols = slice(h * RET_HEAD_DIM, (h + 1) * RET_HEAD_DIM)
            kr = k_ref[rows, cols]
            qb = q_ref[rows, cols].astype(BF16)
            kb = kr.astype(BF16)
            vb = v_ref[rows, cols].astype(BF16)
            scores = lax.dot_general(qb, kb, (((1,), (1,)), ((), ())),
                                     preferred_element_type=F32) * intra_ref[h]
            state = s_ref[0, h]
            o = jnp.dot(scores.astype(BF16), vb, preferred_element_type=F32)
            o = o + jnp.dot(qb, state.astype(BF16), preferred_element_type=F32) * qdec_ref[h]
            kd = (kr * kdec_ref[h]).astype(BF16)
            s_ref[0, h] = state * cdec_ref[h] + lax.dot_general(
                kd, vb, (((0,), (0,)), ((), ())), preferred_element_type=F32)
            ret_cols = slice(POOL_WIDTH + cols.start, POOL_WIDTH + cols.stop)
            mo_ref[rows, ret_cols] = _group_norm_gate(o, g_ref[rows, cols]).astype(BF16)


def _mixer_prompt_call(proj_p, batch, seq, pool_w, pool_scale, tables):
    intra, qdec, kdec, cdec = tables
    assert MIX_TT % intra.shape[1] == 0
    tiles = seq // MIX_TT
    halo_per_tile = MIX_TT // POOL_HALO
    halo_per_seq = seq // POOL_HALO

    return pl.pallas_call(
        _mixer_prompt_kernel,
        out_shape=(jax.ShapeDtypeStruct((batch * seq, MIX_WIDTH), BF16),
                   jax.ShapeDtypeStruct((batch, RET_HEADS, RET_HEAD_DIM, RET_HEAD_DIM), F32),
                   jax.ShapeDtypeStruct((batch, POOL_BUF, POOL_WIDTH), F32)),
        grid=(batch, tiles),
        in_specs=[
            pl.BlockSpec(memory_space=pltpu.SMEM),
            pl.BlockSpec((MIX_TT, IN_COLS), lambda b, i: (b * tiles + i, 0)),
            pl.BlockSpec((POOL_HALO, POOL_WIDTH),
                         lambda b, i: (jnp.maximum(b * halo_per_seq + i * halo_per_tile - 1, 0), 0)),
            pl.BlockSpec(intra.shape, lambda b, i: (0, 0, 0)),
            pl.BlockSpec(qdec.shape, lambda b, i: (0, 0, 0)),
            pl.BlockSpec(kdec.shape, lambda b, i: (0, 0, 0)),
            pl.BlockSpec(pool_w.shape, lambda b, i: (0, 0, 0)),
            pl.BlockSpec((1, POOL_WIDTH), lambda b, i: (0, 0)),
        ],
        out_specs=(
            pl.BlockSpec((MIX_TT, MIX_WIDTH), lambda b, i: (b * tiles + i, 0)),
            pl.BlockSpec((1, RET_HEADS, RET_HEAD_DIM, RET_HEAD_DIM), lambda b, i: (b, 0, 0, 0)),
            pl.BlockSpec((1, POOL_BUF, POOL_WIDTH), lambda b, i: (b, 0, 0)),
        ),
        compiler_params=_params(("arbitrary", "arbitrary")),
        name="mixer_prompt",
    )(cdec, proj_p, proj_p, intra, qdec, kdec, pool_w, pool_scale)


RET_S_BB = 8


def _mixer_sample_kernel(cdec_ref, q_ref, k_ref, v_ref, g_ref, s0_ref, intra_ref, qdec_ref, kdec_ref,
                         st_ref, u_ref, pw_ref, ps_ref, mo_ref, s_ref, nb_ref, *, steps, pos0):
    bb = RET_S_BB
    rows = steps * bb

    keep = POOL_BUF - steps
    nb_ref[0:keep] = st_ref[steps:POOL_BUF]
    for t in range(steps):
        nb_ref[keep + t] = u_ref[t]
    for g, w in enumerate(POOL_WINDOWS):
        cols = slice(g * POOL_GROUP_DIM, (g + 1) * POOL_GROUP_DIM)
        us = [u_ref[t, :, cols] for t in range(steps)]
        ms = []
        for t in range(steps):
            acc = us[t]
            for t2 in range(max(0, t - w + 1), t):
                acc = acc + us[t2]
            for j in range(max(POOL_BUF + t - w + 1, 0), POOL_BUF):
                acc = acc + st_ref[j, :, cols]
            count = float(min(pos0 + t + 1, w))
            ms.append(acc / count - us[t])
        m = jnp.concatenate(ms, axis=0).astype(BF16)
        out = jnp.dot(m, pw_ref[g].astype(BF16), preferred_element_type=F32) * ps_ref[:, cols]
        for t in range(steps):
            mo_ref[t, :, cols] = out[t * bb:(t + 1) * bb, :].astype(BF16)

    row_b = lax.broadcasted_iota(jnp.int32, (rows, RET_HEAD_DIM), 0) % bb

    def gather(ref, cols):
        return jnp.concatenate([ref[t, :, cols] for t in range(steps)], axis=0)

    for h in range(RET_HEADS):
        cols = slice(h * RET_HEAD_DIM, (h + 1) * RET_HEAD_DIM)
        kr = gather(k_ref, cols)
        qb = gather(q_ref, cols).astype(BF16)
        kb = kr.astype(BF16)
        vb = gather(v_ref, cols).astype(BF16)
        scores = lax.dot_general(qb, kb, (((1,), (1,)), ((), ())),
                                 preferred_element_type=F32) * intra_ref[h]
        o = jnp.dot(scores.astype(BF16), vb, preferred_element_type=F32)
        kd = kr * kdec_ref[h]
        inter = jnp.zeros((rows, RET_HEAD_DIM), F32)
        for b in range(bb):
            state = s0_ref[b, h]
            mine = row_b == b
            r = jnp.dot(qb, state.astype(BF16), preferred_element_type=F32)
            inter = jnp.where(mine, r, inter)
            kd_b = jnp.where(mine, kd, 0.0).astype(BF16)
            s_ref[b, h] = state * cdec_ref[h] + lax.dot_general(
                kd_b, vb, (((0,), (0,)), ((), ())), preferred_element_type=F32)
        o = o + inter * qdec_ref[h]
        res = _group_norm_gate(o, gather(g_ref, cols))
        ret_cols = slice(POOL_WIDTH + cols.start, POOL_WIDTH + cols.stop)
        for t in range(steps):
            mo_ref[t, :, ret_cols] = res[t * bb:(t + 1) * bb, :].astype(BF16)


def _mixer_sample_call(proj_s3, state_ret, state_pool_t, layer, pool_w, pool_scale, tables, pos0):
    intra, qdec, kdec, cdec = tables
    steps, nb, _ = proj_s3.shape
    bb = RET_S_BB

    def col(j):
        return pl.BlockSpec((steps, bb, RET_WIDTH), lambda i, j=j: (0, i, j))

    sblock = pl.BlockSpec((bb, RET_HEADS, RET_HEAD_DIM, RET_HEAD_DIM), lambda i: (i, 0, 0, 0))
    return pl.pallas_call(
        functools.partial(_mixer_sample_kernel, steps=steps, pos0=pos0),
        out_shape=(jax.ShapeDtypeStruct((steps, nb, MIX_WIDTH), BF16),
                   jax.ShapeDtypeStruct(state_ret.shape, F32),
                   jax.ShapeDtypeStruct((POOL_BUF, nb, POOL_WIDTH), F32)),
        grid=(nb // bb,),
        in_specs=[
            pl.BlockSpec(memory_space=pltpu.SMEM),
            col(1), col(2), col(3), col(4),
            sblock,
            pl.BlockSpec(intra.shape, lambda i: (0, 0, 0)),
            pl.BlockSpec(qdec.shape, lambda i: (0, 0, 0)),
            pl.BlockSpec(kdec.shape, lambda i: (0, 0, 0)),
            pl.BlockSpec((None, POOL_BUF, bb, POOL_WIDTH), lambda i: (layer, 0, i, 0)),
            col(0),
            pl.BlockSpec(pool_w.shape, lambda i: (0, 0, 0)),
            pl.BlockSpec((1, POOL_WIDTH), lambda i: (0, 0)),
        ],
        out_specs=(
            pl.BlockSpec((steps, bb, MIX_WIDTH), lambda i: (0, i, 0)),
            sblock,
            pl.BlockSpec((POOL_BUF, bb, POOL_WIDTH), lambda i: (0, i, 0)),
        ),
        compiler_params=_params(("arbitrary",)),
        name="mixer_sample",
    )(cdec, proj_s3, proj_s3, proj_s3, proj_s3, state_ret, intra, qdec, kdec,
      state_pool_t, proj_s3, pool_w, pool_scale)


def _rotary_tables(seq, pos0):
    half = RET_HEAD_DIM // 2
    pos = jnp.arange(seq, dtype=F32) + pos0
    inv = ROPE_BASE ** (-jnp.arange(half, dtype=F32) / half)
    ang = pos[:, None] * inv[None, :]
    return jnp.cos(ang), jnp.sin(ang)


def _ret_tables(seq, group):
    chunk = math.gcd(seq, RET_CHUNK)
    log_g = jnp.log(1.0 - jnp.power(2.0, -5.0 - jnp.arange(RET_HEADS, dtype=F32)))
    idx = jnp.arange(chunk, dtype=F32)
    diff = idx[:, None] - idx[None, :]
    intra = jnp.where(diff[None] >= 0, jnp.exp(jnp.maximum(diff, 0.0)[None] * log_g[:, None, None]), 0.0)
    q_dec = jnp.exp((idx + 1.0)[None, :] * log_g[:, None])
    k_dec = jnp.exp((chunk - 1.0 - idx)[None, :] * log_g[:, None])
    c_dec = jnp.exp(chunk * log_g)
    if group > 1:
        assert chunk == seq
        same = jnp.eye(group, dtype=F32)
        intra = (intra[:, :, None, :, None] * same[None, None, :, None, :]).reshape(
            RET_HEADS, chunk * group, chunk * group)
        q_dec = jnp.repeat(q_dec, group, axis=1)
        k_dec = jnp.repeat(k_dec, group, axis=1)
    rows = q_dec.shape[1]
    q_dec = jnp.broadcast_to(q_dec[:, :, None], (RET_HEADS, rows, RET_HEAD_DIM))
    k_dec = jnp.broadcast_to(k_dec[:, :, None], (RET_HEADS, rows, RET_HEAD_DIM))
    return intra, q_dec, k_dec, c_dec


def _out_proj_kernel(mp_ref, ms_ref, xp_ref, xs_ref, gtp_ref, gts_ref, w_ref, op_ref, os_ref):
    mix = jnp.concatenate([mp_ref[...], ms_ref[...]], axis=0)
    m = jnp.dot(mix, w_ref[...].astype(BF16), preferred_element_type=F32)
    op_ref[...] = xp_ref[...] + gtp_ref[0] * m[0:TILE_P]
    os_ref[...] = xs_ref[...] + gts_ref[...] * m[TILE_P:]


def _out_proj_call(mix_p, mix_s, xp, xs, mods_p, mods_s, w_out):
    tn = 1024
    nn = D_MODEL // tn
    ts = xs.shape[0] // N_TILES
    return pl.pallas_call(
        _out_proj_kernel,
        out_shape=(jax.ShapeDtypeStruct(xp.shape, F32), jax.ShapeDtypeStruct(xs.shape, F32)),
        grid=(nn, N_TILES),
        in_specs=[
            pl.BlockSpec((TILE_P, MIX_WIDTH), lambda n, i: (i, 0)),
            pl.BlockSpec((ts, MIX_WIDTH), lambda n, i: (i, 0)),
            pl.BlockSpec((TILE_P, tn), lambda n, i: (i, n)),
            pl.BlockSpec((ts, tn), lambda n, i: (i, n)),
            pl.BlockSpec((1, 1, tn), lambda n, i: (i // 2, 0, 5 * nn + n)),
            pl.BlockSpec((ts, tn), lambda n, i: (i % 2, 5 * nn + n)),
            pl.BlockSpec((MIX_WIDTH, tn), lambda n, i: (0, n)),
        ],
        out_specs=(
            pl.BlockSpec((TILE_P, tn), lambda n, i: (i, n)),
            pl.BlockSpec((ts, tn), lambda n, i: (i, n)),
        ),
        compiler_params=_params(("arbitrary", "arbitrary")),
        name="out_proj",
    )(mix_p, mix_s, xp, xs, mods_p, mods_s, w_out)


@jax.jit
def _step(x_prompt, x_sample, c_prompt, c_sample, state_pool, state_ret,
          ada_w, ada_b, norm_ffn1, ffn1_w_gate, ffn1_w_up, ffn1_w_down,
          norm_mix, w_in, pool_w, pool_scale, w_out,
          norm_ffn2, ffn2_w_gate, ffn2_w_up, ffn2_w_down, norm_final):
    bp, seq, d = x_prompt.shape
    bs, steps, _ = x_sample.shape
    depth = ada_w.shape[0]
    assert d == D_MODEL and bp * seq == N_TILES * TILE_P and seq % TILE_P == 0
    assert seq // TILE_P == 2 and (bs * steps) % (2 * N_TILES) == 0 and bs % RET_S_BB == 0

    xp = x_prompt.reshape(bp * seq, d)
    xs = jnp.transpose(x_sample, (1, 0, 2)).reshape(steps * bs, d)
    pad = (-(bs + bp)) % 16
    c_all = jnp.concatenate([c_sample, c_prompt, jnp.zeros((pad, d), F32)], axis=0)
    state_pool_t = jnp.transpose(state_pool, (0, 2, 1, 3))
    tab_p = _ret_tables(seq, 1)
    tab_s = _ret_tables(steps, RET_S_BB)
    rot_p = _rotary_tables(seq, 0)
    rot_s = tuple(jnp.repeat(t, (bs * steps) // N_TILES, axis=0) for t in _rotary_tables(steps, PAST_LEN))

    pool_prompt, ret_prompt, pool_sample, ret_sample = [], [], [], []
    for l in range(depth):
        mods_s, mods_pp = _ada_call(c_all, bs, ada_w[l], ada_b[l][None, :])
        mods_p = mods_pp[:bp].reshape(bp, 1, N_MOD * d)

        xp, xs = _ffn_call(xp, xs, mods_p, mods_s, 0, norm_ffn1[l][None, :],
                           ffn1_w_gate[l], ffn1_w_up[l], ffn1_w_down[l])

        proj_p, proj_s = _proj_call(xp, xs, mods_p, mods_s, norm_mix[l][None, :], w_in[l], rot_p, rot_s)
        proj_s3 = proj_s.reshape(steps, bs, IN_COLS)
        mix_p, s_p, buf_p = _mixer_prompt_call(proj_p, bp, seq, pool_w[l], pool_scale[l][None, :], tab_p)
        mix_s, s_s, buf_s = _mixer_sample_call(proj_s3, state_ret[l], state_pool_t, l, pool_w[l],
                                               pool_scale[l][None, :], tab_s, PAST_LEN)
        xp, xs = _out_proj_call(mix_p, mix_s.reshape(steps * bs, MIX_WIDTH), xp, xs, mods_p, mods_s,
                                w_out[l])

        last = l == depth - 1
        xp, xs = _ffn_call(xp, xs, mods_p, mods_s, 6, norm_ffn2[l][None, :],
                           ffn2_w_gate[l], ffn2_w_up[l], ffn2_w_down[l],
                           gain_final=norm_final[None, :] if last else None)

        pool_prompt.append(buf_p)
        pool_sample.append(jnp.transpose(buf_s, (1, 0, 2)))
        ret_prompt.append(s_p)
        ret_sample.append(s_s)

    y_prompt = xp.reshape(bp, seq, d)
    y_sample = jnp.transpose(xs.reshape(steps, bs, d), (1, 0, 2))
    return (y_prompt, y_sample, jnp.stack(pool_prompt), jnp.stack(ret_prompt),
            jnp.stack(pool_sample), jnp.stack(ret_sample))


def kernel(x_prompt, x_sample, c_prompt, c_sample, state_pool, state_ret, ada_w, ada_b, norm_ffn1, ffn1_w_gate, ffn1_w_up, ffn1_w_down, norm_mix, w_in, pool_w, pool_scale, w_out, norm_ffn2, ffn2_w_gate, ffn2_w_up, ffn2_w_down, norm_final):
    return _step(x_prompt, x_sample, c_prompt, c_sample, state_pool, state_ret,
                 ada_w, ada_b, norm_ffn1, ffn1_w_gate, ffn1_w_up, ffn1_w_down,
                 norm_mix, w_in, pool_w, pool_scale, w_out,
                 norm_ffn2, ffn2_w_gate, ffn2_w_up, ffn2_w_down, norm_final)
```

```python
import functools
import math

import jax
import jax.numpy as jnp
from jax import lax
from jax.experimental import pallas as pl
from jax.experimental.pallas import tpu as pltpu

F32 = jnp.float32
BF16 = jnp.bfloat16

D_MODEL = 2048
D_FF = 5632
N_MOD = 9
EPS = 1e-6
POOL_WINDOWS = (2, 4, 8, 16)
POOL_WIDTH = D_MODEL // 2
POOL_GROUP_DIM = POOL_WIDTH // len(POOL_WINDOWS)
POOL_BUF = max(POOL_WINDOWS) - 1
MIX_WIDTH = D_MODEL
RET_WIDTH = MIX_WIDTH - POOL_WIDTH
RET_HEADS = 4
RET_HEAD_DIM = RET_WIDTH // RET_HEADS
RET_CHUNK = 128
ROPE_BASE = 10000.0
IN_COLS = POOL_WIDTH + 4 * RET_WIDTH
PAST_LEN = 16384

TILE_P = 1024
N_TILES = 8
VMEM_LIMIT = 62 * 1024 * 1024


def _rms_mod(x, gain, shift, scale):
    y = x * lax.rsqrt(jnp.mean(x * x, axis=-1, keepdims=True) + EPS) * gain
    return y * (1.0 + scale) + shift


def _silu(x):
    return x * jax.nn.sigmoid(x)


ROW_CHUNK = 16
NORM_UNROLL = 16
FINAL_CHUNK = 128


def _fill_h(h_ref, xp_ref, xs_ref, gain, shp, scp, shs, scs, accp_ref=None, accs_ref=None):
    def body(c, carry):
        rows = pl.ds(pl.multiple_of(c * ROW_CHUNK, ROW_CHUNK), ROW_CHUNK)
        x = xp_ref[rows, :]
        h_ref[rows, :] = _rms_mod(x, gain, shp, scp).astype(BF16)
        if accp_ref is not None:
            accp_ref[rows, :] = x
        return carry

    n_p = xp_ref.shape[0]
    lax.fori_loop(0, n_p // ROW_CHUNK, body, 0, unroll=NORM_UNROLL)
    for s in range(0, xs_ref.shape[0], ROW_CHUNK):
        rows = slice(s, s + ROW_CHUNK)
        x = xs_ref[rows, :]
        h_ref[n_p + s:n_p + s + ROW_CHUNK, :] = _rms_mod(x, gain, shs[rows, :], scs[rows, :]).astype(BF16)
        if accs_ref is not None:
            accs_ref[rows, :] = x


def _params(sem, vmem=VMEM_LIMIT):
    return pltpu.CompilerParams(dimension_semantics=sem, vmem_limit_bytes=vmem)


def _ada_kernel(c_ref, w_ref, b_ref, os_ref, op_ref):
    a = _silu(c_ref[...]).astype(BF16)
    r = jnp.dot(a, w_ref[...].astype(BF16), preferred_element_type=F32) + b_ref[...]
    ns = os_ref.shape[0]
    os_ref[...] = r[:ns]
    op_ref[...] = r[ns:]


def _ada_call(c_all, n_sample, ada_w, ada_b):
    rows = c_all.shape[0]
    n = ada_w.shape[1]
    tn = 1024
    return pl.pallas_call(
        _ada_kernel,
        out_shape=(jax.ShapeDtypeStruct((n_sample, n), F32),
                   jax.ShapeDtypeStruct((rows - n_sample, n), F32)),
        grid=(n // tn,),
        in_specs=[
            pl.BlockSpec((rows, D_MODEL), lambda j: (0, 0)),
            pl.BlockSpec((D_MODEL, tn), lambda j: (0, j)),
            pl.BlockSpec((1, tn), lambda j: (0, j)),
        ],
        out_specs=(pl.BlockSpec((n_sample, tn), lambda j: (0, j)),
                   pl.BlockSpec((rows - n_sample, tn), lambda j: (0, j))),
        compiler_params=_params(("arbitrary",)),
        name="ada_mods",
    )(c_all, ada_w, ada_b)


def _ffn_kernel(xp_ref, xs_ref, shp_ref, scp_ref, gtp_ref, shs_ref, scs_ref, gts_ref,
                gain_ref, wg_ref, wu_ref, wd_ref, *rest, final_norm):
    if final_norm:
        gfin_ref, op_ref, os_ref, h_ref = rest
    else:
        op_ref, os_ref, h_ref = rest
    f = pl.program_id(1)

    @pl.when(f == 0)
    def _():
        _fill_h(h_ref, xp_ref, xs_ref, gain_ref[...], shp_ref[0], scp_ref[0], shs_ref, scs_ref,
                accp_ref=op_ref, accs_ref=os_ref)

    h = h_ref[...]
    g = jnp.dot(h, wg_ref[...].astype(BF16), preferred_element_type=F32)
    u = jnp.dot(h, wu_ref[...].astype(BF16), preferred_element_type=F32)
    a = (_silu(g) * u).astype(BF16)
    d = jnp.dot(a, wd_ref[...].astype(BF16), preferred_element_type=F32)
    op_ref[...] += (0.5 * gtp_ref[0]) * d[0:TILE_P]
    os_ref[...] += (0.5 * gts_ref[...]) * d[TILE_P:]

    if final_norm:
        @pl.when(f == pl.num_programs(1) - 1)
        def _():
            gfin = gfin_ref[...]

            def norm(y):
                return y * lax.rsqrt(jnp.mean(y * y, axis=-1, keepdims=True) + EPS) * gfin

            def body(c, carry):
                rows = pl.ds(pl.multiple_of(c * FINAL_CHUNK, FINAL_CHUNK), FINAL_CHUNK)
                op_ref[rows, :] = norm(op_ref[rows, :])
                return carry

            lax.fori_loop(0, TILE_P // FINAL_CHUNK, body, 0)
            os_ref[...] = norm(os_ref[...])


def _ffn_call(xp, xs, mods_p, mods_s, mod0, gain, wg, wu, wd, gain_final=None):
    tf = 256
    ts = xs.shape[0] // N_TILES
    final_norm = gain_final is not None

    def pmod(k):
        return pl.BlockSpec((1, 1, D_MODEL), lambda i, f, k=k: (i // 2, 0, k))

    def smod(k):
        return pl.BlockSpec((ts, D_MODEL), lambda i, f, k=k: (i % 2, k))

    in_specs = [
        pl.BlockSpec((TILE_P, D_MODEL), lambda i, f: (i, 0)),
        pl.BlockSpec((ts, D_MODEL), lambda i, f: (i, 0)),
        pmod(mod0), pmod(mod0 + 1), pmod(mod0 + 2),
        smod(mod0), smod(mod0 + 1), smod(mod0 + 2),
        pl.BlockSpec((1, D_MODEL), lambda i, f: (0, 0)),
        pl.BlockSpec((D_MODEL, tf), lambda i, f: (0, f)),
        pl.BlockSpec((D_MODEL, tf), lambda i, f: (0, f)),
        pl.BlockSpec((tf, D_MODEL), lambda i, f: (f, 0)),
    ]
    args = [xp, xs, mods_p, mods_p, mods_p, mods_s, mods_s, mods_s, gain, wg, wu, wd]
    if final_norm:
        in_specs.append(pl.BlockSpec((1, D_MODEL), lambda i, f: (0, 0)))
        args.append(gain_final)
    return pl.pallas_call(
        functools.partial(_ffn_kernel, final_norm=final_norm),
        out_shape=(jax.ShapeDtypeStruct(xp.shape, F32), jax.ShapeDtypeStruct(xs.shape, F32)),
        grid=(N_TILES, D_FF // tf),
        in_specs=in_specs,
        out_specs=(
            pl.BlockSpec((TILE_P, D_MODEL), lambda i, f: (i, 0)),
            pl.BlockSpec((ts, D_MODEL), lambda i, f: (i, 0)),
        ),
        scratch_shapes=[pltpu.VMEM((TILE_P + ts, D_MODEL), BF16)],
        compiler_params=_params(("arbitrary", "arbitrary")),
        name="ffn_final" if final_norm else "ffn",
    )(*args)


def _rotary(x, cos, sin):
    half = RET_HEAD_DIM // 2
    x1, x2 = x[:, :half], x[:, half:]
    return jnp.concatenate([x1 * cos - x2 * sin, x2 * cos + x1 * sin], axis=1)


PROJ_TN = 1024


def _proj_kernel(xp_ref, xs_ref, shp_ref, scp_ref, shs_ref, scs_ref, gain_ref, w_ref,
                 cosp_ref, sinp_ref, coss_ref, sins_ref, op_ref, os_ref, h_ref):
    n = pl.program_id(1)
    sub = pl.program_id(2)

    @pl.when(n == 0)
    def _():
        _fill_h(h_ref.at[sub], xp_ref, xs_ref, gain_ref[...], shp_ref[0], scp_ref[0], shs_ref, scs_ref)

    def project():
        return jnp.dot(h_ref[sub], w_ref[...].astype(BF16), preferred_element_type=F32)

    rotate = (n == 1) | (n == 2)

    @pl.when(jnp.logical_not(rotate))
    def _():
        r = project()
        op_ref[...] = r[0:TILE_P]
        os_ref[...] = r[TILE_P:]

    @pl.when(rotate)
    def _():
        r = project()
        scale = jnp.where(n == 2, RET_HEAD_DIM ** -0.5, 1.0).astype(F32)
        cos = jnp.concatenate([cosp_ref[...], coss_ref[...]], axis=0)
        sin = jnp.concatenate([sinp_ref[...], sins_ref[...]], axis=0)
        for h in range(RET_HEADS):
            cols = slice(h * RET_HEAD_DIM, (h + 1) * RET_HEAD_DIM)
            y = _rotary(r[:, cols], cos, sin) * scale
            op_ref[:, cols] = y[0:TILE_P]
            os_ref[:, cols] = y[TILE_P:]


def _proj_call(xp, xs, mods_p, mods_s, gain, w_in, rot_p, rot_s):
    tn = PROJ_TN
    assert tn == POOL_WIDTH == RET_WIDTH
    ts = xs.shape[0] // N_TILES
    half = RET_HEAD_DIM // 2

    def tile(p, n, s):
        return 2 * p + jnp.where(n == 0, s, 1)

    def rot_rows(p, n, s):
        return jnp.where((n == 1) | (n == 2), s, 1)

    return pl.pallas_call(
        _proj_kernel,
        out_shape=(jax.ShapeDtypeStruct((xp.shape[0], IN_COLS), F32),
                   jax.ShapeDtypeStruct((xs.shape[0], IN_COLS), F32)),
        grid=(N_TILES // 2, IN_COLS // tn, 2),
        in_specs=[
            pl.BlockSpec((TILE_P, D_MODEL), lambda p, n, s: (tile(p, n, s), 0)),
            pl.BlockSpec((ts, D_MODEL), lambda p, n, s: (tile(p, n, s), 0)),
            pl.BlockSpec((1, 1, D_MODEL), lambda p, n, s: (p, 0, 3)),
            pl.BlockSpec((1, 1, D_MODEL), lambda p, n, s: (p, 0, 4)),
            pl.BlockSpec((ts, D_MODEL), lambda p, n, s: (tile(0, n, s), 3)),
            pl.BlockSpec((ts, D_MODEL), lambda p, n, s: (tile(0, n, s), 4)),
            pl.BlockSpec((1, D_MODEL), lambda p, n, s: (0, 0)),
            pl.BlockSpec((D_MODEL, tn), lambda p, n, s: (0, n)),
            pl.BlockSpec((TILE_P, half), lambda p, n, s: (rot_rows(p, n, s), 0)),
            pl.BlockSpec((TILE_P, half), lambda p, n, s: (rot_rows(p, n, s), 0)),
            pl.BlockSpec((ts, half), lambda p, n, s: (p, 0)),
            pl.BlockSpec((ts, half), lambda p, n, s: (p, 0)),
        ],
        out_specs=(
            pl.BlockSpec((TILE_P, tn), lambda p, n, s: (2 * p + s, n)),
            pl.BlockSpec((ts, tn), lambda p, n, s: (2 * p + s, n)),
        ),
        scratch_shapes=[pltpu.VMEM((2, TILE_P + ts, D_MODEL), BF16)],
        compiler_params=_params(("arbitrary", "arbitrary", "arbitrary")),
        name="mix_proj",
    )(xp, xs, mods_p, mods_p, mods_s, mods_s, gain, w_in, *rot_p, *rot_s)


POOL_HALO = 16


def _group_norm_gate(o, g):
    o = o * lax.rsqrt(jnp.mean(o * o, axis=-1, keepdims=True) + EPS)
    return _silu(g) * o


MIX_TT = 512


def _mixer_prompt_kernel(cdec_ref, proj_ref, halo_ref,
                         intra_ref, qdec_ref, kdec_ref, pw_ref, ps_ref, mo_ref, s_ref, nb_ref):
    i = pl.program_id(1)
    u_ref = proj_ref.at[:, 0:POOL_WIDTH]
    q_ref, k_ref, v_ref, g_ref = (
        proj_ref.at[:, POOL_WIDTH + j * RET_WIDTH:POOL_WIDTH + (j + 1) * RET_WIDTH] for j in range(4))

    @pl.when(i == pl.num_programs(1) - 1)
    def _():
        nb_ref[0] = u_ref[MIX_TT - POOL_BUF:MIX_TT, :]

    @pl.when(i == 0)
    def _():
        s_ref[...] = jnp.zeros_like(s_ref)

    halo = jnp.where(i == 0, 0.0, halo_ref[...])
    u = u_ref[...]
    ue = jnp.concatenate([halo, u], axis=0)
    pos = (lax.broadcasted_iota(jnp.int32, (MIX_TT, POOL_GROUP_DIM), 0) + i * MIX_TT).astype(F32)
    for g, w in enumerate(POOL_WINDOWS):
        cols = slice(g * POOL_GROUP_DIM, (g + 1) * POOL_GROUP_DIM)
        s = ue[:, cols]
        off = 0
        k = 1
        while k < w:
            s = s[k:, :] + s[:-k, :]
            off += k
            k *= 2
        wsum = s[POOL_HALO - off:POOL_HALO - off + MIX_TT, :]
        count = jnp.minimum(pos + 1.0, float(w))
        m = wsum / count - u[:, cols]
        out = jnp.dot(m.astype(BF16), pw_ref[g].astype(BF16), preferred_element_type=F32)
        mo_ref[:, cols] = (out * ps_ref[:, cols]).astype(BF16)

    chunk = intra_ref.shape[1]
    for c in range(MIX_TT // chunk):
        rows = slice(c * chunk, (c + 1) * chunk)
        for h in range(RET_HEADS):
            cols = slice(h * RET_HEAD_DIM, (h + 1) * RET_HEAD_DIM)
            kr = k_ref[rows, cols]
            qb = q_ref[rows, cols].astype(BF16)
            kb = kr.astype(BF16)
            vb = v_ref[rows, cols].astype(BF16)
            scores = lax.dot_general(qb, kb, (((1,), (1,)), ((), ())),
                                     preferred_element_type=F32) * intra_ref[h]
            state = s_ref[0, h]
            o = jnp.dot(scores.astype(BF16), vb, preferred_element_type=F32)
            o = o + jnp.dot(qb, state.astype(BF16), preferred_element_type=F32) * qdec_ref[h]
            kd = (kr * kdec_ref[h]).astype(BF16)
            s_ref[0, h] = state * cdec_ref[h] + lax.dot_general(
                kd, vb, (((0,), (0,)), ((), ())), preferred_element_type=F32)
            ret_cols = slice(POOL_WIDTH + cols.start, POOL_WIDTH + cols.stop)
            mo_ref[rows, ret_cols] = _group_norm_gate(o, g_ref[rows, cols]).astype(BF16)


def _mixer_prompt_call(proj_p, batch, seq, pool_w, pool_scale, tables):
    intra, qdec, kdec, cdec = tables
    assert MIX_TT % intra.shape[1] == 0
    tiles = seq // MIX_TT
    halo_per_tile = MIX_TT // POOL_HALO
    halo_per_seq = seq // POOL_HALO

    return pl.pallas_call(
        _mixer_prompt_kernel,
        out_shape=(jax.ShapeDtypeStruct((batch * seq, MIX_WIDTH), BF16),
                   jax.ShapeDtypeStruct((batch, RET_HEADS, RET_HEAD_DIM, RET_HEAD_DIM), F32),
                   jax.ShapeDtypeStruct((batch, POOL_BUF, POOL_WIDTH), F32)),
        grid=(batch, tiles),
        in_specs=[
            pl.BlockSpec(memory_space=pltpu.SMEM),
            pl.BlockSpec((MIX_TT, IN_COLS), lambda b, i: (b * tiles + i, 0)),
            pl.BlockSpec((POOL_HALO, POOL_WIDTH),
                         lambda b, i: (jnp.maximum(b * halo_per_seq + i * halo_per_tile - 1, 0), 0)),
            pl.BlockSpec(intra.shape, lambda b, i: (0, 0, 0)),
            pl.BlockSpec(qdec.shape, lambda b, i: (0, 0, 0)),
            pl.BlockSpec(kdec.shape, lambda b, i: (0, 0, 0)),
            pl.BlockSpec(pool_w.shape, lambda b, i: (0, 0, 0)),
            pl.BlockSpec((1, POOL_WIDTH), lambda b, i: (0, 0)),
        ],
        out_specs=(
            pl.BlockSpec((MIX_TT, MIX_WIDTH), lambda b, i: (b * tiles + i, 0)),
            pl.BlockSpec((1, RET_HEADS, RET_HEAD_DIM, RET_HEAD_DIM), lambda b, i: (b, 0, 0, 0)),
            pl.BlockSpec((1, POOL_BUF, POOL_WIDTH), lambda b, i: (b, 0, 0)),
        ),
        compiler_params=_params(("arbitrary", "arbitrary")),
        name="mixer_prompt",
    )(cdec, proj_p, proj_p, intra, qdec, kdec, pool_w, pool_scale)


RET_S_BB = 8


def _mixer_sample_kernel(cdec_ref, q_ref, k_ref, v_ref, g_ref, s0_ref, intra_ref, qdec_ref, kdec_ref,
                         st_ref, u_ref, pw_ref, ps_ref, mo_ref, s_ref, nb_ref, *, steps, pos0):
    bb = RET_S_BB
    rows = steps * bb

    keep = POOL_BUF - steps
    nb_ref[0:keep] = st_ref[steps:POOL_BUF]
    for t in range(steps):
        nb_ref[keep + t] = u_ref[t]
    for g, w in enumerate(POOL_WINDOWS):
        cols = slice(g * POOL_GROUP_DIM, (g + 1) * POOL_GROUP_DIM)
        us = [u_ref[t, :, cols] for t in range(steps)]
        ms = []
        for t in range(steps):
            acc = us[t]
            for t2 in range(max(0, t - w + 1), t):
                acc = acc + us[t2]
            for j in range(max(POOL_BUF + t - w + 1, 0), POOL_BUF):
                acc = acc + st_ref[j, :, cols]
            count = float(min(pos0 + t + 1, w))
            ms.append(acc / count - us[t])
        m = jnp.concatenate(ms, axis=0).astype(BF16)
        out = jnp.dot(m, pw_ref[g].astype(BF16), preferred_element_type=F32) * ps_ref[:, cols]
        for t in range(steps):
            mo_ref[t, :, cols] = out[t * bb:(t + 1) * bb, :].astype(BF16)

    row_b = lax.broadcasted_iota(jnp.int32, (rows, RET_HEAD_DIM), 0) % bb

    def gather(ref, cols):
        return jnp.concatenate([ref[t, :, cols] for t in range(steps)], axis=0)

    for h in range(RET_HEADS):
        cols = slice(h * RET_HEAD_DIM, (h + 1) * RET_HEAD_DIM)
        kr = gather(k_ref, cols)
        qb = gather(q_ref, cols).astype(BF16)
        kb = kr.astype(BF16)
        vb = gather(v_ref, cols).astype(BF16)
        scores = lax.dot_general(qb, kb, (((1,), (1,)), ((), ())),
                                 preferred_element_type=F32) * intra_ref[h]
        o = jnp.dot(scores.astype(BF16), vb, preferred_element_type=F32)
        kd = kr * kdec_ref[h]
        inter = jnp.zeros((rows, RET_HEAD_DIM), F32)
        for b in range(bb):
            state = s0_ref[b, h]
            mine = row_b == b
            r = jnp.dot(qb, state.astype(BF16), preferred_element_type=F32)
            inter = jnp.where(mine, r, inter)
            kd_b = jnp.where(mine, kd, 0.0).astype(BF16)
            s_ref[b, h] = state * cdec_ref[h] + lax.dot_general(
                kd_b, vb, (((0,), (0,)), ((), ())), preferred_element_type=F32)
        o = o + inter * qdec_ref[h]
        res = _group_norm_gate(o, gather(g_ref, cols))
        ret_cols = slice(POOL_WIDTH + cols.start, POOL_WIDTH + cols.stop)
        for t in range(steps):
            mo_ref[t, :, ret_cols] = res[t * bb:(t + 1) * bb, :].astype(BF16)


def _mixer_sample_call(proj_s3, state_ret, state_pool_t, layer, pool_w, pool_scale, tables, pos0):
    intra, qdec, kdec, cdec = tables
    steps, nb, _ = proj_s3.shape
    bb = RET_S_BB

    def col(j):
        return pl.BlockSpec((steps, bb, RET_WIDTH), lambda i, j=j: (0, i, j))

    sblock = pl.BlockSpec((bb, RET_HEADS, RET_HEAD_DIM, RET_HEAD_DIM), lambda i: (i, 0, 0, 0))
    return pl.pallas_call(
        functools.partial(_mixer_sample_kernel, steps=steps, pos0=pos0),
        out_shape=(jax.ShapeDtypeStruct((steps, nb, MIX_WIDTH), BF16),
                   jax.ShapeDtypeStruct(state_ret.shape, F32),
                   jax.ShapeDtypeStruct((POOL_BUF, nb, POOL_WIDTH), F32)),
        grid=(nb // bb,),
        in_specs=[
            pl.BlockSpec(memory_space=pltpu.SMEM),
            col(1), col(2), col(3), col(4),
            sblock,
            pl.BlockSpec(intra.shape, lambda i: (0, 0, 0)),
            pl.BlockSpec(qdec.shape, lambda i: (0, 0, 0)),
            pl.BlockSpec(kdec.shape, lambda i: (0, 0, 0)),
            pl.BlockSpec((None, POOL_BUF, bb, POOL_WIDTH), lambda i: (layer, 0, i, 0)),
            col(0),
            pl.BlockSpec(pool_w.shape, lambda i: (0, 0, 0)),
            pl.BlockSpec((1, POOL_WIDTH), lambda i: (0, 0)),
        ],
        out_specs=(
            pl.BlockSpec((steps, bb, MIX_WIDTH), lambda i: (0, i, 0)),
            sblock,
            pl.BlockSpec((POOL_BUF, bb, POOL_WIDTH), lambda i: (0, i, 0)),
        ),
        compiler_params=_params(("arbitrary",)),
        name="mixer_sample",
    )(cdec, proj_s3, proj_s3, proj_s3, proj_s3, state_ret, intra, qdec, kdec,
      state_pool_t, proj_s3, pool_w, pool_scale)


def _rotary_tables(seq, pos0):
    half = RET_HEAD_DIM // 2
    pos = jnp.arange(seq, dtype=F32) + pos0
    inv = ROPE_BASE ** (-jnp.arange(half, dtype=F32) / half)
    ang = pos[:, None] * inv[None, :]
    return jnp.cos(ang), jnp.sin(ang)


def _ret_tables(seq, group):
    chunk = math.gcd(seq, RET_CHUNK)
    log_g = jnp.log(1.0 - jnp.power(2.0, -5.0 - jnp.arange(RET_HEADS, dtype=F32)))
    idx = jnp.arange(chunk, dtype=F32)
    diff = idx[:, None] - idx[None, :]
    intra = jnp.where(diff[None] >= 0, jnp.exp(jnp.maximum(diff, 0.0)[None] * log_g[:, None, None]), 0.0)
    q_dec = jnp.exp((idx + 1.0)[None, :] * log_g[:, None])
    k_dec = jnp.exp((chunk - 1.0 - idx)[None, :] * log_g[:, None])
    c_dec = jnp.exp(chunk * log_g)
    if group > 1:
        assert chunk == seq
        same = jnp.eye(group, dtype=F32)
        intra = (intra[:, :, None, :, None] * same[None, None, :, None, :]).reshape(
            RET_HEADS, chunk * group, chunk * group)
        q_dec = jnp.repeat(q_dec, group, axis=1)
        k_dec = jnp.repeat(k_dec, group, axis=1)
    rows = q_dec.shape[1]
    q_dec = jnp.broadcast_to(q_dec[:, :, None], (RET_HEADS, rows, RET_HEAD_DIM))
    k_dec = jnp.broadcast_to(k_dec[:, :, None], (RET_HEADS, rows, RET_HEAD_DIM))
    return intra, q_dec, k_dec, c_dec


def _out_proj_kernel(mp_ref, ms_ref, xp_ref, xs_ref, gtp_ref, gts_ref, w_ref, op_ref, os_ref):
    mix = jnp.concatenate([mp_ref[...], ms_ref[...]], axis=0)
    m = jnp.dot(mix, w_ref[...].astype(BF16), preferred_element_type=F32)
    op_ref[...] = xp_ref[...] + gtp_ref[0] * m[0:TILE_P]
    os_ref[...] = xs_ref[...] + gts_ref[...] * m[TILE_P:]


def _out_proj_call(mix_p, mix_s, xp, xs, mods_p, mods_s, w_out):
    tn = 1024
    nn = D_MODEL // tn
    ts = xs.shape[0] // N_TILES
    return pl.pallas_call(
        _out_proj_kernel,
        out_shape=(jax.ShapeDtypeStruct(xp.shape, F32), jax.ShapeDtypeStruct(xs.shape, F32)),
        grid=(nn, N_TILES),
        in_specs=[
            pl.BlockSpec((TILE_P, MIX_WIDTH), lambda n, i: (i, 0)),
            pl.BlockSpec((ts, MIX_WIDTH), lambda n, i: (i, 0)),
            pl.BlockSpec((TILE_P, tn), lambda n, i: (i, n)),
            pl.BlockSpec((ts, tn), lambda n, i: (i, n)),
            pl.BlockSpec((1, 1, tn), lambda n, i: (i // 2, 0, 5 * nn + n)),
            pl.BlockSpec((ts, tn), lambda n, i: (i % 2, 5 * nn + n)),
            pl.BlockSpec((MIX_WIDTH, tn), lambda n, i: (0, n)),
        ],
        out_specs=(
            pl.BlockSpec((TILE_P, tn), lambda n, i: (i, n)),
            pl.BlockSpec((ts, tn), lambda n, i: (i, n)),
        ),
        compiler_params=_params(("arbitrary", "arbitrary")),
        name="out_proj",
    )(mix_p, mix_s, xp, xs, mods_p, mods_s, w_out)


@jax.jit
def _step(x_prompt, x_sample, c_prompt, c_sample, state_pool, state_ret,
          ada_w, ada_b, norm_ffn1, ffn1_w_gate, ffn1_w_up, ffn1_w_down,
          norm_mix, w_in, pool_w, pool_scale, w_out,
          norm_ffn2, ffn2_w_gate, ffn2_w_up, ffn2_w_down, norm_final):
    bp, seq, d = x_prompt.shape
    bs, steps, _ = x_sample.shape
    depth = ada_w.shape[0]
    assert d == D_MODEL and bp * seq == N_TILES * TILE_P and seq % TILE_P == 0
    assert seq // TILE_P == 2 and (bs * steps) % (2 * N_TILES) == 0 and bs % RET_S_BB == 0

    xp = x_prompt.reshape(bp * seq, d)
    xs = jnp.transpose(x_sample, (1, 0, 2)).reshape(steps * bs, d)
    pad = (-(bs + bp)) % 16
    c_all = jnp.concatenate([c_sample, c_prompt, jnp.zeros((pad, d), F32)], axis=0)
    state_pool_t = jnp.transpose(state_pool, (0, 2, 1, 3))
    tab_p = _ret_tables(seq, 1)
    tab_s = _ret_tables(steps, RET_S_BB)
    rot_p = _rotary_tables(seq, 0)
    rot_s = tuple(jnp.repeat(t, (bs * steps) // N_TILES, axis=0) for t in _rotary_tables(steps, PAST_LEN))

    pool_prompt, ret_prompt, pool_sample, ret_sample = [], [], [], []
    for l in range(depth):
        mods_s, mods_pp = _ada_call(c_all, bs, ada_w[l], ada_b[l][None, :])
        mods_p = mods_pp[:bp].reshape(bp, 1, N_MOD * d)

        xp, xs = _ffn_call(xp, xs, mods_p, mods_s, 0, norm_ffn1[l][None, :],
                           ffn1_w_gate[l], ffn1_w_up[l], ffn1_w_down[l])

        proj_p, proj_s = _proj_call(xp, xs, mods_p, mods_s, norm_mix[l][None, :], w_in[l], rot_p, rot_s)
        proj_s3 = proj_s.reshape(steps, bs, IN_COLS)
        mix_p, s_p, buf_p = _mixer_prompt_call(proj_p, bp, seq, pool_w[l], pool_scale[l][None, :], tab_p)
        mix_s, s_s, buf_s = _mixer_sample_call(proj_s3, state_ret[l], state_pool_t, l, pool_w[l],
                                               pool_scale[l][None, :], tab_s, PAST_LEN)
        xp, xs = _out_proj_call(mix_p, mix_s.reshape(steps * bs, MIX_WIDTH), xp, xs, mods_p, mods_s,
                                w_out[l])

        last = l == depth - 1
        xp, xs = _ffn_call(xp, xs, mods_p, mods_s, 6, norm_ffn2[l][None, :],
                           ffn2_w_gate[l], ffn2_w_up[l], ffn2_w_down[l],
                           gain_final=norm_final[None, :] if last else None)

        pool_prompt.append(buf_p)
        pool_sample.append(jnp.transpose(buf_s, (1, 0, 2)))
        ret_prompt.append(s_p)
        ret_sample.append(s_s)

    y_prompt = xp.reshape(bp, seq, d)
    y_sample = jnp.transpose(xs.reshape(steps, bs, d), (1, 0, 2))
    return (y_prompt, y_sample, jnp.stack(pool_prompt), jnp.stack(ret_prompt),
            jnp.stack(pool_sample), jnp.stack(ret_sample))


def kernel(x_prompt, x_sample, c_prompt, c_sample, state_pool, state_ret, ada_w, ada_b, norm_ffn1, ffn1_w_gate, ffn1_w_up, ffn1_w_down, norm_mix, w_in, pool_w, pool_scale, w_out, norm_ffn2, ffn2_w_gate, ffn2_w_up, ffn2_w_down, norm_final):
    return _step(x_prompt, x_sample, c_prompt, c_sample, state_pool, state_ret,
                 ada_w, ada_b, norm_ffn1, ffn1_w_gate, ffn1_w_up, ffn1_w_down,
                 norm_mix, w_in, pool_w, pool_scale, w_out,
                 norm_ffn2, ffn2_w_gate, ffn2_w_up, ffn2_w_down, norm_final)
```
